```python
import math
import jax
import jax.numpy as jnp
from jax import lax
import numpy as np

D_MODEL = 2048
BATCH = 2
SEQ = 16384
DEPTH = 1

ATT_HEADS = 8
ATT_HEAD_DIM = 128
ATT_WIDTH = ATT_HEADS * ATT_HEAD_DIM
DILATED_BRANCHES = ((128, 1), (512, 4), (2048, 16))
BAND_BLOCK = 128
ROPE_THETA = 500000.0
ROPE_DIM = ATT_HEAD_DIM // 4
ML_HEADS = 4
ML_QK_DIM = 128
ML_V_DIM = 256
ML_QK_WIDTH = ML_HEADS * ML_QK_DIM
ML_V_WIDTH = ML_HEADS * ML_V_DIM
ML_CHUNK = 128
ML_CONV = 4
ML_F_BIAS = 3.0
MIX_WIDTH = ATT_WIDTH + ML_V_WIDTH
SPLITS = (ATT_WIDTH, ATT_WIDTH, ATT_WIDTH, ML_QK_WIDTH, ML_QK_WIDTH, ML_V_WIDTH, ML_V_WIDTH, ML_HEADS, ML_HEADS)
IN_WIDTH = sum(SPLITS)
SPLIT_POINTS = [int(s) for s in np.cumsum(SPLITS)[:-1]]
N_EXPERTS = 32
TOP_K = 4
D_FF = D_MODEL
SWIGLU_ALPHA = 1.702
SWIGLU_LIMIT = 7.0
MOE_ROWS = 512
NORM_EPS = 1e-6

kernel_name = "hybrid_dilated_attn_mlstm_moe"


def rms_norm(x, g):
    xf = x.astype(jnp.float32)
    y = xf * lax.rsqrt(jnp.mean(xf * xf, axis=-1, keepdims=True) + NORM_EPS)
    return y * g.astype(jnp.float32)


def partial_rope(x, pos):
    half = ROPE_DIM // 2
    inv = ROPE_THETA ** (-jnp.arange(half, dtype=jnp.float32) * 2.0 / ROPE_DIM)
    ang = pos[:, None] * inv[None, :]
    cos = jnp.cos(ang)[None, :, None, :]
    sin = jnp.sin(ang)[None, :, None, :]
    x1 = x[..., :half]
    x2 = x[..., half:ROPE_DIM]
    return jnp.concatenate([x1 * cos - x2 * sin, x2 * cos + x1 * sin, x[..., ROPE_DIM:]], axis=-1)


def dilated_branch(q, k, v, window, dilation):
    n_w = window // dilation
    assert n_w <= BAND_BLOCK
    B, S, H, E = q.shape
    seg = dilation * BAND_BLOCK
    Sp = -(-S // seg) * seg
    nb = Sp // seg
    pad = ((0, 0), (0, Sp - S), (0, 0), (0, 0))

    def split(t):
        return jnp.pad(t, pad).reshape(B, nb, BAND_BLOCK, dilation, H, E)

    def with_prev(t):
        prev = jnp.pad(t[:, :-1], ((0, 0), (1, 0), (0, 0), (0, 0), (0, 0), (0, 0)))
        return jnp.concatenate([prev, t], axis=2)

    qb = split(q)
    kk = with_prev(split(k))
    vv = with_prev(split(v))
    s = jnp.einsum('bnirhe,bnjrhe->bnrhij', qb, kk) / math.sqrt(E)
    i = jnp.arange(BAND_BLOCK)[:, None]
    j = jnp.arange(2 * BAND_BLOCK)[None, :]
    dist = BAND_BLOCK + i - j
    blk = jnp.arange(nb)[:, None, None]
    valid = (dist >= 0) & (dist <= n_w) & ((blk > 0) | (j >= BAND_BLOCK))
    s = jnp.where(valid[None, :, None, None], s, -jnp.inf)
    m = jnp.max(s, axis=-1, keepdims=True)
    p = jnp.exp(s - m)
    l = jnp.sum(p, axis=-1, keepdims=True)
    o = jnp.einsum('bnrhij,bnjrhe->bnirhe', p / l, vv).reshape(B, Sp, H, E)[:, :S]
    lse = (m + jnp.log(l))[..., 0]
    lse = lse.transpose(0, 1, 4, 2, 3).reshape(B, Sp, H)[:, :S]
    return o, lse


def dilated_attention(q, k, v):
    outs, lses = [], []
    for window, dilation in DILATED_BRANCHES:
        o, lse = dilated_branch(q, k, v, window, dilation)
        outs.append(o)
        lses.append(lse)
    w = jax.nn.softmax(jnp.stack(lses, axis=0), axis=0)
    return jnp.sum(w[..., None] * jnp.stack(outs, axis=0), axis=0)


def causal_conv(x, w, b):
    C = x.shape[-1]
    out = lax.conv_general_dilated(x, w.astype(x.dtype)[:, None, :], window_strides=(1,),
                                   padding=((ML_CONV - 1, 0),),
                                   dimension_numbers=('NWC', 'WIO', 'NWC'),
                                   feature_group_count=C)
    return out + b.astype(x.dtype)


def mlstm_chunkwise(q, k, v, ig, fg):
    B, S, H, DK = q.shape
    DV = v.shape[-1]
    L = ML_CHUNK
    NC = S // L

    def chunk(t):
        return t.reshape(B, NC, L, H, -1).transpose(0, 3, 1, 2, 4)

    q = chunk(q)
    k = chunk(k) / math.sqrt(DK)
    v = chunk(v)
    ig = ig.reshape(B, NC, L, H).transpose(0, 3, 1, 2)
    logf = jax.nn.log_sigmoid(fg.reshape(B, NC, L, H).transpose(0, 3, 1, 2))
    b = jnp.cumsum(logf, axis=-1)
    g = b[..., -1]
    a = g[..., None] - b + ig
    m_loc = jnp.max(a, axis=-1)
    wk = jnp.exp(a - m_loc[..., None])
    c_loc = jnp.einsum('bhcsk,bhcsv->bhckv', wk[..., None] * k, v)
    n_loc = jnp.einsum('bhcs,bhcsk->bhck', wk, k)

    def step(carry, inp):
        c, n, m = carry
        g_c, m_l, c_l, n_l = inp
        m_new = jnp.maximum(g_c + m, m_l)
        d_old = jnp.exp(g_c + m - m_new)
        d_new = jnp.exp(m_l - m_new)
        c2 = d_old[..., None, None] * c + d_new[..., None, None] * c_l
        n2 = d_old[..., None] * n + d_new[..., None] * n_l
        return (c2, n2, m_new), (c, n, m)

    init = (jnp.zeros((B, H, DK, DV), jnp.float32), jnp.zeros((B, H, DK), jnp.float32),
            jnp.zeros((B, H), jnp.float32))
    xs = (jnp.moveaxis(g, 2, 0), jnp.moveaxis(m_loc, 2, 0), jnp.moveaxis(c_loc, 2, 0),
          jnp.moveaxis(n_loc, 2, 0))
    _, (c_prev, n_prev, m_prev) = lax.scan(step, init, xs)
    c_prev = jnp.moveaxis(c_prev, 0, 2)
    n_prev = jnp.moveaxis(n_prev, 0, 2)
    m_prev = jnp.moveaxis(m_prev, 0, 2)

    causal = jnp.tril(jnp.ones((L, L), dtype=bool))
    dmat = jnp.where(causal, b[..., :, None] - b[..., None, :] + ig[..., None, :], -jnp.inf)
    inter = b + m_prev[..., None]
    m_t = jnp.maximum(inter, jnp.max(dmat, axis=-1))
    pw = jnp.exp(dmat - m_t[..., None]) * jnp.einsum('bhctk,bhcsk->bhcts', q, k)
    wi = jnp.exp(inter - m_t)
    num = wi[..., None] * jnp.einsum('bhctk,bhckv->bhctv', q, c_prev) + jnp.einsum('bhcts,bhcsv->bhctv', pw, v)
    den = wi * jnp.einsum('bhctk,bhck->bhct', q, n_prev) + jnp.sum(pw, axis=-1)
    h = num / jnp.maximum(jnp.abs(den), jnp.exp(-m_t))[..., None]
    return h.transpose(0, 2, 3, 1, 4).reshape(B, S, H, DV)


def moe_ffn(xf, router_w, router_b, w_up, b_up, w_down, b_down):
    N, D = xf.shape
    R = MOE_ROWS
    logits = xf.astype(jnp.float32) @ router_w.astype(jnp.float32) + router_b.astype(jnp.float32)
    vals, idx = lax.top_k(logits, TOP_K)
    gates = jax.nn.softmax(vals, axis=-1)
    e_flat = idx.reshape(-1).astype(jnp.int32)
    tok_flat = (jnp.arange(N * TOP_K, dtype=jnp.int32) // TOP_K)
    w_flat = gates.reshape(-1)
    order = jnp.argsort(e_flat, stable=True)
    e_s = e_flat[order]
    tok_s = tok_flat[order]
    w_s = w_flat[order]
    counts = jnp.bincount(e_flat, length=N_EXPERTS)
    start = jnp.cumsum(counts) - counts
    pcounts = (counts + R - 1) // R * R
    pend = jnp.cumsum(pcounts)
    pstart = pend - pcounts
    rank = jnp.arange(N * TOP_K, dtype=jnp.int32) - start[e_s]
    dest = pstart[e_s] + rank
    NB = -(-(N * TOP_K) // R) + N_EXPERTS
    row_tok = jnp.full((NB * R,), N, dtype=jnp.int32).at[dest].set(tok_s)
    row_w = jnp.zeros((NB * R,), jnp.float32).at[dest].set(w_s)
    blk_e = jnp.minimum(jnp.searchsorted(pend, jnp.arange(NB) * R, side='right'), N_EXPERTS - 1).astype(jnp.int32)
    x_pad = jnp.concatenate([xf, jnp.zeros((1, D), xf.dtype)], axis=0)

    def step(acc, inp):
        toks, ws, e = inp
        h = x_pad[toks] @ w_up[e] + b_up[e]
        glu = jnp.minimum(h[:, ::2].astype(jnp.float32), SWIGLU_LIMIT)
        lin = jnp.clip(h[:, 1::2].astype(jnp.float32), -SWIGLU_LIMIT, SWIGLU_LIMIT)
        act = (glu * jax.nn.sigmoid(SWIGLU_ALPHA * glu) * (lin + 1.0)).astype(xf.dtype)
        y = (act @ w_down[e] + b_down[e]).astype(jnp.float32)
        return acc.at[toks].add(y * ws[:, None]), None

    acc, _ = lax.scan(step, jnp.zeros((N + 1, D), jnp.float32),
                      (row_tok.reshape(NB, R), row_w.reshape(NB, R), blk_e))
    return acc[:N]


def setup_inputs(seed: int = 0) -> dict:
    key = jax.random.key(seed)
    ks = jax.random.split(key, 20)
    f32 = jnp.float32
    nrm = lambda k, s: jax.random.normal(k, s, f32)
    return {
        "x": nrm(ks[0], (BATCH, SEQ, D_MODEL)),
        "attn_norm_g": 1.0 + 0.02 * nrm(ks[1], (DEPTH, D_MODEL)),
        "w_in": nrm(ks[2], (DEPTH, D_MODEL, IN_WIDTH)) * D_MODEL ** -0.5,
        "q_norm_g": 1.0 + 0.02 * nrm(ks[3], (DEPTH, ATT_HEAD_DIM)),
        "k_norm_g": 1.0 + 0.02 * nrm(ks[4], (DEPTH, ATT_HEAD_DIM)),
        "ml_conv_w": nrm(ks[5], (DEPTH, ML_CONV, 2 * ML_QK_WIDTH)) * ML_CONV ** -0.5,
        "ml_conv_b": 0.01 * nrm(ks[6], (DEPTH, 2 * ML_QK_WIDTH)),
        "ml_i_b": 0.1 * nrm(ks[7], (DEPTH, ML_HEADS)),
        "ml_f_b": ML_F_BIAS + 0.1 * nrm(ks[8], (DEPTH, ML_HEADS)),
        "ml_out_norm_g": 1.0 + 0.02 * nrm(ks[9], (DEPTH, ML_V_WIDTH)),
        "w_out": nrm(ks[10], (DEPTH, MIX_WIDTH, D_MODEL)) * MIX_WIDTH ** -0.5,
        "ffn_norm_g": 1.0 + 0.02 * nrm(ks[11], (DEPTH, D_MODEL)),
        "router_w": nrm(ks[12], (DEPTH, D_MODEL, N_EXPERTS)) * D_MODEL ** -0.5,
        "router_b": 0.01 * nrm(ks[13], (DEPTH, N_EXPERTS)),
        "w_up": nrm(ks[14], (DEPTH, N_EXPERTS, D_MODEL, 2 * D_FF)) * D_MODEL ** -0.5,
        "b_up": 0.01 * nrm(ks[15], (DEPTH, N_EXPERTS, 2 * D_FF)),
        "w_down": nrm(ks[16], (DEPTH, N_EXPERTS, D_FF, D_MODEL)) * D_FF ** -0.5,
        "b_down": 0.01 * nrm(ks[17], (DEPTH, N_EXPERTS, D_MODEL)),
    }


def reference(x, attn_norm_g, w_in, q_norm_g, k_norm_g, ml_conv_w, ml_conv_b, ml_i_b, ml_f_b,
              ml_out_norm_g, w_out, ffn_norm_g, router_w, router_b, w_up, b_up, w_down, b_down):
    B, S, D = x.shape
    pos = jnp.arange(S, dtype=jnp.float32)
    for l in range(DEPTH):
        h = rms_norm(x, attn_norm_g[l]).astype(x.dtype)
        proj = h @ w_in[l]
        aq, ak, av, mq, mk, mv, mo, mi, mf = jnp.split(proj, SPLIT_POINTS, axis=-1)
        aq = partial_rope(rms_norm(aq.reshape(B, S, ATT_HEADS, ATT_HEAD_DIM), q_norm_g[l]), pos)
        ak = partial_rope(rms_norm(ak.reshape(B, S, ATT_HEADS, ATT_HEAD_DIM), k_norm_g[l]), pos)
        av = av.astype(jnp.float32).reshape(B, S, ATT_HEADS, ATT_HEAD_DIM)
        att = dilated_attention(aq, ak, av).reshape(B, S, ATT_WIDTH)
        qk = jax.nn.silu(causal_conv(jnp.concatenate([mq, mk], axis=-1).astype(jnp.float32),
                                     ml_conv_w[l], ml_conv_b[l]))
        mq = qk[..., :ML_QK_WIDTH].reshape(B, S, ML_HEADS, ML_QK_DIM)
        mk = qk[..., ML_QK_WIDTH:].reshape(B, S, ML_HEADS, ML_QK_DIM)
        mv = mv.astype(jnp.float32).reshape(B, S, ML_HEADS, ML_V_DIM)
        ig = mi.astype(jnp.float32) + ml_i_b[l].astype(jnp.float32)
        fg = mf.astype(jnp.float32) + ml_f_b[l].astype(jnp.float32)
        hm = mlstm_chunkwise(mq, mk, mv, ig, fg)
        hm = rms_norm(hm, ml_out_norm_g[l].reshape(ML_HEADS, ML_V_DIM))
        hm = hm * jax.nn.sigmoid(mo.astype(jnp.float32)).reshape(B, S, ML_HEADS, ML_V_DIM)
        mix = jnp.concatenate([att, hm.reshape(B, S, ML_V_WIDTH)], axis=-1).astype(x.dtype)
        x = x + mix @ w_out[l]
        h2 = rms_norm(x, ffn_norm_g[l]).astype(x.dtype)
        y = moe_ffn(h2.reshape(B * S, D), router_w[l], router_b[l], w_up[l], b_up[l], w_down[l], b_down[l])
        x = x + y.reshape(B, S, D).astype(x.dtype)
    return x
```

```python
import functools
import math

import jax
import jax.numpy as jnp
from jax import lax
from jax.experimental import pallas as pl
from jax.experimental.pallas import tpu as pltpu

F32 = jnp.float32
BF16 = jnp.bfloat16
I32 = jnp.int32

NORM_EPS = 1e-6
ATT_HEADS = 8
HEAD_DIM = 128
ROPE_DIM = 32
ROPE_THETA = 500000.0
DILATIONS = (1, 4, 16)
BAND = 128
ATT_TILE = 2048
ML_HEADS = 4
ML_QK = 128
ML_V = 256
ML_CHUNK = 128
ML_CONV = 4
N_EXPERTS = 32
TOP_K = 4
SWIGLU_ALPHA = 1.702
SWIGLU_LIMIT = 7.0
NEG = -1e30
LANES = 128
MIB = 2 ** 20


def _params(n_axes, vmem_mib):
    return pltpu.CompilerParams(dimension_semantics=("arbitrary",) * n_axes,
                                vmem_limit_bytes=vmem_mib * MIB)


def _inproj_body(x_ref, g_ref, w_ref, wg_ref, o_ref, gate_ref, h_ref):
    @pl.when(pl.program_id(1) == 0)
    def _():
        xf = x_ref[...]
        ms = jnp.mean(xf * xf, axis=-1, keepdims=True)
        h = (xf * lax.rsqrt(ms + NORM_EPS) * g_ref[...]).astype(BF16)
        h_ref[...] = h
        gate_ref[...] = jnp.dot(h, wg_ref[...], preferred_element_type=F32)

    o_ref[...] = jnp.dot(h_ref[...], w_ref[...], preferred_element_type=F32)


def _inproj(x2d, g, w_main, w_gate, tm, tn):
    n, d = x2d.shape
    wcols = w_main.shape[1]
    return pl.pallas_call(
        _inproj_body,
        grid=(n // tm, wcols // tn),
        in_specs=[pl.BlockSpec((tm, d), lambda i, j: (i, 0)),
                  pl.BlockSpec((1, d), lambda i, j: (0, 0)),
                  pl.BlockSpec((d, tn), lambda i, j: (0, j)),
                  pl.BlockSpec((d, LANES), lambda i, j: (0, 0))],
        out_specs=[pl.BlockSpec((tm, tn), lambda i, j: (i, j)),
                   pl.BlockSpec((tm, LANES), lambda i, j: (i, 0))],
        out_shape=[jax.ShapeDtypeStruct((n, wcols), F32),
                   jax.ShapeDtypeStruct((n, LANES), F32)],
        scratch_shapes=[pltpu.VMEM((tm, d), BF16)],
        compiler_params=_params(2, 48),
        name="inproj",
    )(x2d, g, w_main, w_gate)


def _attn_body(q_ref, kc_ref, kp_ref, vc_ref, vp_ref, gq_ref, gk_ref, tc_ref, tp_ref, o_ref,
               qb, kb, vb, acc_s, m_s, l_s):
    t = pl.program_id(1)
    T = ATT_TILE
    CH = 256
    scale = 1.0 / math.sqrt(HEAD_DIM)

    def norm_rope(x, g, tab):
        ms = jnp.mean(x * x, axis=-1, keepdims=True)
        y = x * lax.rsqrt(ms + NORM_EPS) * g
        c = tab[:, 0:LANES]
        s1 = tab[:, LANES:2 * LANES]
        s2 = tab[:, 2 * LANES:3 * LANES]
        return y * c + pltpu.roll(y, LANES - ROPE_DIM // 2, 1) * s1 + pltpu.roll(y, ROPE_DIM // 2, 1) * s2

    def prep(c, carry):
        rows = pl.ds(pl.multiple_of(c * CH, CH), CH)
        prow = pl.ds(pl.multiple_of(c * CH + T, CH), CH)
        tabc = tc_ref[rows, :]
        qb[rows, :] = norm_rope(q_ref[rows, :], gq_ref[...], tabc) * scale
        kb[prow, :] = norm_rope(kc_ref[rows, :], gk_ref[...], tabc)
        kb[rows, :] = norm_rope(kp_ref[rows, :], gk_ref[...], tp_ref[rows, :])
        vb[rows, :] = vp_ref[rows, :]
        vb[prow, :] = vc_ref[rows, :]
        return carry

    lax.fori_loop(0, T // CH, prep, 0)

    def unit(bidx, d, u):
        seg = BAND * d
        if d == 1:
            sg, qoff = u, pl.multiple_of(u * BAND, BAND)
        elif d == DILATIONS[-1]:
            sg, qoff = 0, u
        else:
            sg = lax.shift_right_logical(u, 2)
            qoff = sg * seg + lax.bitwise_and(u, d - 1)

        def rows(off):
            return pl.ds(off, BAND) if d == 1 else pl.ds(off, BAND, stride=d)

        qs = qb[rows(qoff), :].astype(BF16)
        kk = jnp.concatenate([kb[rows(T + qoff - seg), :], kb[rows(T + qoff), :]], axis=0).astype(BF16)
        vv = jnp.concatenate([vb[rows(T + qoff - seg), :], vb[rows(T + qoff), :]], axis=0).astype(BF16)
        s = lax.dot_general(qs, kk, (((1,), (1,)), ((), ())), preferred_element_type=F32)
        ii = lax.broadcasted_iota(I32, (BAND, 2 * BAND), 0)
        jj = lax.broadcasted_iota(I32, (BAND, 2 * BAND), 1)
        dist = BAND + ii - jj
        first = jnp.logical_and(t == 0, sg == 0)
        jmin = jnp.where(first, BAND, 0)
        valid = (dist >= 0) & (dist <= BAND) & (jj >= jmin)
        s = jnp.where(valid, s, NEG)
        m = jnp.max(s, axis=-1, keepdims=True)
        p = jnp.exp(s - m)
        l = jnp.sum(p, axis=-1, keepdims=True)
        acc = jnp.dot(p.astype(BF16), vv, preferred_element_type=F32)
        acc_s.at[bidx][rows(qoff), :] = acc
        m_s.at[bidx][rows(qoff), :] = jnp.broadcast_to(m, (BAND, LANES))
        l_s.at[bidx][rows(qoff), :] = jnp.broadcast_to(l, (BAND, LANES))

    for bidx, d in enumerate(DILATIONS):
        def body(u, carry, bidx=bidx, d=d):
            unit(bidx, d, u)
            return carry
        lax.fori_loop(0, T // BAND, body, 0)

    def mix(c, carry):
        rows = pl.ds(pl.multiple_of(c * CH, CH), CH)
        ms = [m_s[b, rows, :] for b in range(3)]
        mx = jnp.maximum(jnp.maximum(ms[0], ms[1]), ms[2])
        num = jnp.zeros((CH, LANES), F32)
        den = jnp.zeros((CH, LANES), F32)
        for b in range(3):
            w = jnp.exp(ms[b] - mx)
            num = num + w * acc_s[b, rows, :]
            den = den + w * l_s[b, rows, :]
        o_ref[rows, :] = (num / den).astype(o_ref.dtype)
        return carry

    lax.fori_loop(0, T // CH, mix, 0)


def _attention(proj, gq, gk, tab, batch, seq):
    n = proj.shape[0]
    T = ATT_TILE
    nt = seq // T
    H = ATT_HEADS

    def cur(col0):
        return pl.BlockSpec((T, HEAD_DIM), lambda b, t, h: (b * nt + t, col0 + h))

    def prev(col0):
        return pl.BlockSpec((T, HEAD_DIM), lambda b, t, h: (b * nt + jnp.maximum(t - 1, 0), col0 + h))

    return pl.pallas_call(
        _attn_body,
        grid=(batch, nt, H),
        in_specs=[cur(0), cur(H), prev(H), cur(2 * H), prev(2 * H),
                  pl.BlockSpec((1, HEAD_DIM), lambda b, t, h: (0, 0)),
                  pl.BlockSpec((1, HEAD_DIM), lambda b, t, h: (0, 0)),
                  pl.BlockSpec((T, 3 * LANES), lambda b, t, h: (t, 0)),
                  pl.BlockSpec((T, 3 * LANES), lambda b, t, h: (jnp.maximum(t - 1, 0), 0))],
        out_specs=pl.BlockSpec((T, HEAD_DIM), lambda b, t, h: (b * nt + t, h)),
        out_shape=jax.ShapeDtypeStruct((n, H * HEAD_DIM), BF16),
        scratch_shapes=[pltpu.VMEM((T, HEAD_DIM), F32),
                        pltpu.VMEM((2 * T, HEAD_DIM), F32),
                        pltpu.VMEM((2 * T, HEAD_DIM), F32),
                        pltpu.VMEM((3, T, HEAD_DIM), F32),
                        pltpu.VMEM((3, T, LANES), F32),
                        pltpu.VMEM((3, T, LANES), F32)],
        compiler_params=_params(3, 56),
        name="dilated_attn",
    )(proj, proj, proj, proj, proj, gq, gk, tab, tab)


def _rope_table(seq):
    half = ROPE_DIM // 2
    inv = ROPE_THETA ** (-jnp.arange(half, dtype=F32) * 2.0 / ROPE_DIM)
    ang = jnp.arange(seq, dtype=F32)[:, None] * inv[None, :]
    cos, sin = jnp.cos(ang), jnp.sin(ang)
    z = lambda w: jnp.zeros((seq, w), F32)
    c = jnp.concatenate([cos, cos, jnp.ones((seq, LANES - ROPE_DIM), F32)], axis=1)
    s1 = jnp.concatenate([-sin, z(LANES - half)], axis=1)
    s2 = jnp.concatenate([z(half), sin, z(LANES - ROPE_DIM)], axis=1)
    return jnp.concatenate([c, s1, s2], axis=1)


def _mlstm_body(mq_ref, mk_ref, mv_ref, mo_ref, gt_ref, gb_ref, cw_ref, cb_ref, og_ref, o_ref,
                prev_s, c_s, n_s, m_s):
    L = ML_CHUNK
    QW = ML_HEADS * ML_QK

    @pl.when(pl.program_id(1) == 0)
    def _():
        prev_s[...] = jnp.zeros_like(prev_s)
        c_s[...] = jnp.zeros_like(c_s)
        n_s[...] = jnp.zeros_like(n_s)
        m_s[...] = jnp.zeros_like(m_s)

    cur = jnp.concatenate([mq_ref[...], mk_ref[...]], axis=1)
    prv = prev_s[...]
    rid = lax.broadcasted_iota(I32, (L, 2 * QW), 0)
    conv = cur * cw_ref[ML_CONV - 1:ML_CONV, :] + cb_ref[...]
    for s in range(1, ML_CONV):
        shifted = jnp.where(rid < s, pltpu.roll(prv, s, 0), pltpu.roll(cur, s, 0))
        conv = conv + shifted * cw_ref[ML_CONV - 1 - s:ML_CONV - s, :]
    prev_s[...] = cur
    qk = conv * jax.nn.sigmoid(conv)

    gates = gt_ref[...] + gb_ref[...]
    logf = jnp.minimum(gates, 0.0) - jnp.log(1.0 + jnp.exp(-jnp.abs(gates)))
    ti = lax.broadcasted_iota(I32, (L, L), 0)
    si = lax.broadcasted_iota(I32, (L, L), 1)
    causal = si <= ti
    tril = jnp.where(causal, 1.0, 0.0).astype(F32)
    bmat = jnp.dot(tril, logf, precision=lax.Precision.HIGHEST, preferred_element_type=F32)
    rmat = gates - pltpu.roll(bmat, LANES - ML_HEADS, 1)
    rmat_t = rmat.T

    for h in range(ML_HEADS):
        q = qk[:, h * ML_QK:(h + 1) * ML_QK]
        k = qk[:, QW + h * ML_QK:QW + (h + 1) * ML_QK] * (1.0 / math.sqrt(ML_QK))
        v = mv_ref[:, h * ML_V:(h + 1) * ML_V]
        qb, kb, vb = q.astype(BF16), k.astype(BF16), v.astype(BF16)
        b_col = bmat[:, ML_HEADS + h:ML_HEADS + h + 1]
        r_col = rmat[:, h:h + 1]
        r_row = rmat_t[h:h + 1, :]
        g_tot = b_col[L - 1:L, :]
        m_prev = m_s[h]
        c_prev = c_s[h]
        n_prev = n_s[h]

        dmat = jnp.where(causal, b_col + r_row, NEG)
        inter = b_col + m_prev
        m_t = jnp.maximum(inter, jnp.max(dmat, axis=-1, keepdims=True))
        qkt = lax.dot_general(qb, kb, (((1,), (1,)), ((), ())), preferred_element_type=F32)
        pw = jnp.exp(dmat - m_t) * qkt
        wi = jnp.exp(inter - m_t)
        num = wi * jnp.dot(qb, c_prev.astype(BF16), preferred_element_type=F32) \
            + jnp.dot(pw.astype(BF16), vb, preferred_element_type=F32)
        den = wi * jnp.sum(q * n_prev, axis=-1, keepdims=True) + jnp.sum(pw, axis=-1, keepdims=True)
        hh = num / jnp.maximum(jnp.abs(den), jnp.exp(-m_t))
        ms = jnp.mean(hh * hh, axis=-1, keepdims=True)
        hn = hh * lax.rsqrt(ms + NORM_EPS) * og_ref[:, h * ML_V:(h + 1) * ML_V]
        hn = hn * jax.nn.sigmoid(mo_ref[:, h * ML_V:(h + 1) * ML_V])
        o_ref[:, h * ML_V:(h + 1) * ML_V] = hn.astype(o_ref.dtype)

        a_col = g_tot + r_col
        m_loc = jnp.max(a_col, axis=0, keepdims=True)
        wk = jnp.exp(a_col - m_loc)
        wkk = wk * k
        c_loc = lax.dot_general(wkk.astype(BF16), vb, (((0,), (0,)), ((), ())), preferred_element_type=F32)
        n_loc = jnp.sum(wkk, axis=0, keepdims=True)
        m_new = jnp.maximum(g_tot + m_prev, m_loc)
        d_old = jnp.exp(g_tot + m_prev - m_new)
        d_new = jnp.exp(m_loc - m_new)
        c_s[h] = d_old * c_prev + d_new * c_loc
        n_s[h] = d_old * n_prev + d_new * n_loc
        m_s[h] = m_new


def _mlstm(proj, gates, gate_bias, conv_w, conv_b, out_g, batch, seq):
    n = proj.shape[0]
    L = ML_CHUNK
    nc = seq // L
    QW = ML_HEADS * ML_QK
    VW = ML_HEADS * ML_V
    q_col = 3 * ATT_HEADS * HEAD_DIM
    assert q_col % QW == 0 and (q_col + 2 * QW) % VW == 0
    qi = q_col // QW
    vi = (q_col + 2 * QW) // VW
    row = lambda b, c: b * nc + c
    return pl.pallas_call(
        _mlstm_body,
        grid=(batch, nc),
        in_specs=[pl.BlockSpec((L, QW), lambda b, c: (row(b, c), qi)),
                  pl.BlockSpec((L, QW), lambda b, c: (row(b, c), qi + 1)),
                  pl.BlockSpec((L, VW), lambda b, c: (row(b, c), vi)),
                  pl.BlockSpec((L, VW), lambda b, c: (row(b, c), vi + 1)),
                  pl.BlockSpec((L, LANES), lambda b, c: (row(b, c), 0)),
                  pl.BlockSpec((1, LANES), lambda b, c: (0, 0)),
                  pl.BlockSpec((ML_CONV, 2 * QW), lambda b, c: (0, 0)),
                  pl.BlockSpec((1, 2 * QW), lambda b, c: (0, 0)),
                  pl.BlockSpec((1, VW), lambda b, c: (0, 0))],
        out_specs=pl.BlockSpec((L, VW), lambda b, c: (row(b, c), 0)),
        out_shape=jax.ShapeDtypeStruct((n, VW), BF16),
        scratch_shapes=[pltpu.VMEM((L, 2 * QW), F32),
                        pltpu.VMEM((ML_HEADS, ML_QK, ML_V), F32),
                        pltpu.VMEM((ML_HEADS, 1, ML_QK), F32),
                        pltpu.VMEM((ML_HEADS, 1, 1), F32)],
        compiler_params=_params(2, 32),
        name="mlstm",
    )(proj, proj, proj, proj, gates, gate_bias, conv_w, conv_b, out_g)


def _outproj_body(att_ref, hm_ref, x_ref, wa_ref, wm_ref, g_ref, rw_ref, rb_ref,
                  x1_ref, h2_ref, idx_ref, gate_ref, rank_ref, cnt_ref, carry):
    tm = x_ref.shape[0]

    @pl.when(pl.program_id(0) == 0)
    def _():
        carry[...] = jnp.zeros_like(carry)

    y = jnp.dot(att_ref[...], wa_ref[...], preferred_element_type=F32)
    y = y + jnp.dot(hm_ref[...], wm_ref[...], preferred_element_type=F32)
    x1 = x_ref[...] + y
    x1_ref[...] = x1
    ms = jnp.mean(x1 * x1, axis=-1, keepdims=True)
    h2 = x1 * lax.rsqrt(ms + NORM_EPS) * g_ref[...]
    h2_ref[...] = h2
    logits = jnp.dot(h2, rw_ref[...], precision=lax.Precision.HIGHEST,
                     preferred_element_type=F32) + rb_ref[...]
    lane = lax.broadcasted_iota(I32, (tm, LANES), 1)
    work = jnp.where(lane < N_EXPERTS, logits, NEG)
    vals, idxs = [], []
    onehot = jnp.zeros((tm, LANES), F32)
    for _ in range(TOP_K):
        mx = jnp.max(work, axis=-1, keepdims=True)
        ix = jnp.min(jnp.where(work == mx, lane, LANES), axis=-1, keepdims=True)
        sel = lane == ix
        onehot = jnp.where(sel, 1.0, onehot)
        work = jnp.where(sel, 2.0 * NEG, work)
        vals.append(mx)
        idxs.append(ix)
    es = [jnp.exp(v - vals[0]) for v in vals]
    tot = es[0] + es[1] + es[2] + es[3]

    ri = lax.broadcasted_iota(I32, (tm, tm), 0)
    ci = lax.broadcasted_iota(I32, (tm, tm), 1)
    strict = jnp.where(ci < ri, 1.0, 0.0).astype(BF16)
    cum = jnp.dot(strict, onehot.astype(BF16), preferred_element_type=F32) + carry[0:1, :]
    idx_out = jnp.zeros((tm, LANES), I32)
    gate_out = jnp.zeros((tm, LANES), F32)
    rank_out = jnp.zeros((tm, LANES), I32)
    for k in range(TOP_K):
        rk = jnp.sum(jnp.where(lane == idxs[k], cum, 0.0), axis=-1, keepdims=True)
        idx_out = jnp.where(lane == k, idxs[k], idx_out)
        gate_out = jnp.where(lane == k, es[k] / tot, gate_out)
        rank_out = jnp.where(lane == k, rk.astype(I32), rank_out)
    idx_ref[...] = idx_out
    gate_ref[...] = gate_out
    rank_ref[...] = rank_out
    new = carry[...] + jnp.sum(onehot, axis=0, keepdims=True)
    carry[...] = new
    cnt_ref[...] = new


def _outproj(att, hm, x2d, wa, wm, g, rw, rb, tm):
    n, d = x2d.shape
    row = lambda i: (i, 0)
    const = lambda i: (0, 0)
    return pl.pallas_call(
        _outproj_body,
        grid=(n // tm,),
        in_specs=[pl.BlockSpec((tm, att.shape[1]), row),
                  pl.BlockSpec((tm, hm.shape[1]), row),
                  pl.BlockSpec((tm, d), row),
                  pl.BlockSpec(wa.shape, const),
                  pl.BlockSpec(wm.shape, const),
                  pl.BlockSpec((1, d), const),
                  pl.BlockSpec((d, LANES), const),
                  pl.BlockSpec((1, LANES), const)],
        out_specs=[pl.BlockSpec((tm, d), row),
                   pl.BlockSpec((tm, d), row),
                   pl.BlockSpec((tm, LANES), row),
                   pl.BlockSpec((tm, LANES), row),
                   pl.BlockSpec((tm, LANES), row),
                   pl.BlockSpec((8, LANES), const)],
        out_shape=[jax.ShapeDtypeStruct((n, d), F32),
                   jax.ShapeDtypeStruct((n, d), F32),
                   jax.ShapeDtypeStruct((n, LANES), I32),
                   jax.ShapeDtypeStruct((n, LANES), F32),
                   jax.ShapeDtypeStruct((n, LANES), I32),
                   jax.ShapeDtypeStruct((8, LANES), F32)],
        scratch_shapes=[pltpu.VMEM((8, LANES), F32)],
        compiler_params=_params(1, 56),
        name="outproj_router",
    )(att, hm, x2d, wa, wm, g, rw, rb)


def _moe_body(te_ref, tv_ref, to_ref, rowtok_hbm, h2_hbm, wg_ref, wl_ref, bg_ref, bl_ref, wd_ref, bd_ref,
              ys_ref, idx_smem, xf_buf, xb_buf, sem_idx, sem_rows):
    i = pl.program_id(0)
    f = pl.program_id(1)
    tm = xf_buf.shape[0]
    valid = tv_ref[i] > 0

    @pl.when(jnp.logical_and(valid, f == 0))
    def _():
        cp = pltpu.make_async_copy(rowtok_hbm.at[i], idx_smem, sem_idx)
        cp.start()
        cp.wait()

        def issue(r, carry):
            tok = idx_smem[r]
            pltpu.make_async_copy(h2_hbm.at[pl.ds(tok, 1), :], xf_buf.at[pl.ds(r, 1), :], sem_rows).start()
            return carry

        lax.fori_loop(0, tm, issue, 0)
        pltpu.make_async_copy(h2_hbm.at[pl.ds(0, tm), :], xf_buf, sem_rows).wait()
        xb_buf[...] = xf_buf[...].astype(BF16)

    @pl.when(valid)
    def _():
        x = xb_buf[...]
        hg = jnp.dot(x, wg_ref[0], preferred_element_type=F32) + bg_ref[0]
        hl = jnp.dot(x, wl_ref[0], preferred_element_type=F32) + bl_ref[0]
        glu = jnp.minimum(hg, SWIGLU_LIMIT)
        lin = jnp.clip(hl, -SWIGLU_LIMIT, SWIGLU_LIMIT)
        act = glu * jax.nn.sigmoid(SWIGLU_ALPHA * glu) * (lin + 1.0)
        y = jnp.dot(act.astype(BF16), wd_ref[0], preferred_element_type=F32)

        @pl.when(f == 0)
        def _():
            ys_ref[...] = y + bd_ref[0]

        @pl.when(f > 0)
        def _():
            ys_ref[...] += y


def _moe(tile_e, tile_valid, tile_out, row_tok, h2, wg, wl, bg, bl, wd, bd, tm, fc):
    nt = row_tok.shape[0]
    n, d = h2.shape
    ff = wg.shape[2]
    nf = ff // fc

    def fidx(i, f, tv):
        return jnp.where(tv[i] > 0, f, nf - 1)

    grid_spec = pltpu.PrefetchScalarGridSpec(
        num_scalar_prefetch=3,
        grid=(nt, nf),
        in_specs=[pl.BlockSpec(memory_space=pl.ANY),
                  pl.BlockSpec(memory_space=pl.ANY),
                  pl.BlockSpec((1, d, fc), lambda i, f, te, tv, to: (te[i], 0, fidx(i, f, tv))),
                  pl.BlockSpec((1, d, fc), lambda i, f, te, tv, to: (te[i], 0, fidx(i, f, tv))),
                  pl.BlockSpec((1, 1, fc), lambda i, f, te, tv, to: (te[i], 0, fidx(i, f, tv))),
                  pl.BlockSpec((1, 1, fc), lambda i, f, te, tv, to: (te[i], 0, fidx(i, f, tv))),
                  pl.BlockSpec((1, fc, d), lambda i, f, te, tv, to: (te[i], fidx(i, f, tv), 0)),
                  pl.BlockSpec((1, 1, d), lambda i, f, te, tv, to: (te[i], 0, 0))],
        out_specs=pl.BlockSpec((tm, d), lambda i, f, te, tv, to: (to[i], 0)),
        scratch_shapes=[pltpu.SMEM((tm,), I32),
                        pltpu.VMEM((tm, d), F32),
                        pltpu.VMEM((tm, d), BF16),
                        pltpu.SemaphoreType.DMA,
                        pltpu.SemaphoreType.DMA],
    )
    return pl.pallas_call(
        _moe_body,
        grid_spec=grid_spec,
        out_shape=jax.ShapeDtypeStruct((nt * tm, d), F32),
        compiler_params=_params(2, 56),
        name="moe_ffn",
    )(tile_e, tile_valid, tile_out, row_tok, h2, wg, wl, bg, bl, wd, bd)


def _combine_body(dest_hbm, ys_hbm, x1_ref, gate_ref, o_ref, idx_smem, buf, sem_idx, sem_rows):
    i = pl.program_id(0)
    tm = x1_ref.shape[0]
    cp = pltpu.make_async_copy(dest_hbm.at[i], idx_smem, sem_idx)
    cp.start()
    cp.wait()
    for k in range(TOP_K):
        def issue(r, carry, k=k):
            row = idx_smem[k * tm + r]
            pltpu.make_async_copy(ys_hbm.at[pl.ds(row, 1), :], buf.at[k, pl.ds(r, 1), :], sem_rows).start()
            return carry
        lax.fori_loop(0, tm, issue, 0)
    for k in range(TOP_K):
        pltpu.make_async_copy(ys_hbm.at[pl.ds(0, tm), :], buf.at[k], sem_rows).wait()
    out = x1_ref[...]
    g = gate_ref[...]
    for k in range(TOP_K):
        out = out + g[:, k:k + 1] * buf[k]
    o_ref[...] = out


def _combine(dest_tiles, ys, x1, gates, tm):
    n, d = x1.shape
    return pl.pallas_call(
        _combine_body,
        grid=(n // tm,),
        in_specs=[pl.BlockSpec(memory_space=pl.ANY),
                  pl.BlockSpec(memory_space=pl.ANY),
                  pl.BlockSpec((tm, d), lambda i: (i, 0)),
                  pl.BlockSpec((tm, LANES), lambda i: (i, 0))],
        out_specs=pl.BlockSpec((tm, d), lambda i: (i, 0)),
        out_shape=jax.ShapeDtypeStruct((n, d), F32),
        scratch_shapes=[pltpu.SMEM((TOP_K * tm,), I32),
                        pltpu.VMEM((TOP_K, tm, d), F32),
                        pltpu.SemaphoreType.DMA,
                        pltpu.SemaphoreType.DMA],
        compiler_params=_params(1, 40),
        name="moe_combine",
    )(dest_tiles, ys, x1, gates)


def _pad_lanes(a, width=LANES):
    return jnp.pad(a, [(0, 0)] * (a.ndim - 1) + [(0, width - a.shape[-1])])


def kernel(x, attn_norm_g, w_in, q_norm_g, k_norm_g, ml_conv_w, ml_conv_b, ml_i_b, ml_f_b, ml_out_norm_g,
           w_out, ffn_norm_g, router_w, router_b, w_up, b_up, w_down, b_down):
    B, S, D = x.shape
    N = B * S
    assert S % ATT_TILE == 0 and attn_norm_g.shape[0] == 1
    att_w = ATT_HEADS * HEAD_DIM
    main_w = 3 * att_w + 2 * ML_HEADS * ML_QK + 2 * ML_HEADS * ML_V
    x2d = x.reshape(N, D)

    w_in0 = w_in[0]
    w_main = w_in0[:, :main_w].astype(BF16)
    w_gate = _pad_lanes(w_in0[:, main_w:]).astype(BF16)
    proj, gates = _inproj(x2d, attn_norm_g[0][None, :], w_main, w_gate, tm=1024, tn=1024)

    att = _attention(proj, q_norm_g[0][None, :], k_norm_g[0][None, :], _rope_table(S), B, S)

    gate_bias = _pad_lanes(jnp.concatenate([ml_i_b[0], ml_f_b[0]])[None, :]).astype(F32)
    hm = _mlstm(proj, gates, gate_bias, ml_conv_w[0], ml_conv_b[0][None, :], ml_out_norm_g[0][None, :], B, S)

    w_out0 = w_out[0].astype(BF16)
    x1, h2, top_idx, top_gate, top_rank, counts = _outproj(
        att, hm, x2d, w_out0[:att_w], w_out0[att_w:], ffn_norm_g[0][None, :],
        _pad_lanes(router_w[0]).astype(F32), _pad_lanes(router_b[0][None, :]).astype(F32), tm=512)

    tm_moe = 512
    n_tiles = N * TOP_K // tm_moe + N_EXPERTS
    cnt = counts[0, :N_EXPERTS].astype(I32)
    ptiles = (cnt + tm_moe - 1) // tm_moe
    tend = jnp.cumsum(ptiles)
    total = tend[-1]
    pstart = (tend - ptiles) * tm_moe
    e_idx = top_idx[:, :TOP_K]
    dest = pstart[e_idx] + top_rank[:, :TOP_K]
    tok = jnp.broadcast_to(jnp.arange(N, dtype=I32)[:, None], (N, TOP_K))
    row_tok = jnp.zeros((n_tiles * tm_moe,), I32).at[dest.reshape(-1)].set(tok.reshape(-1))
    tile_ids = jnp.arange(n_tiles, dtype=I32)
    tile_e = jnp.minimum(jnp.searchsorted(tend, tile_ids, side="right"), N_EXPERTS - 1).astype(I32)
    tile_valid = (tile_ids < total).astype(I32)
    tile_out = jnp.minimum(tile_ids, total - 1).astype(I32)

    w_up0 = w_up[0]
    wg = w_up0[:, :, 0::2].astype(BF16)
    wl = w_up0[:, :, 1::2].astype(BF16)
    bg = b_up[0][:, None, 0::2].astype(F32)
    bl = b_up[0][:, None, 1::2].astype(F32)
    wd = w_down[0].astype(BF16)
    bd = b_down[0][:, None, :].astype(F32)
    ys = _moe(tile_e, tile_valid, tile_out, row_tok.reshape(n_tiles, tm_moe), h2, wg, wl, bg, bl, wd, bd,
              tm=tm_moe, fc=1024)

    tm_c = 256
    dest_tiles = dest.reshape(N // tm_c, tm_c, TOP_K).transpose(0, 2, 1).reshape(N // tm_c, TOP_K * tm_c)
    out = _combine(dest_tiles, ys, x1, top_gate, tm=tm_c)
    return out.reshape(B, S, D)
```

```python
import functools
import math

import jax
import jax.numpy as jnp
from jax import lax
from jax.experimental import pallas as pl
from jax.experimental.pallas import tpu as pltpu

F32 = jnp.float32
BF16 = jnp.bfloat16
I32 = jnp.int32

NORM_EPS = 1e-6
ATT_HEADS = 8
HEAD_DIM = 128
ROPE_DIM = 32
ROPE_THETA = 500000.0
DILATIONS = (1, 4, 16)
BAND = 128
ATT_TILE = 2048
ML_HEADS = 4
ML_QK = 128
ML_V = 256
ML_CHUNK = 128
ML_CONV = 4
N_EXPERTS = 32
TOP_K = 4
SWIGLU_ALPHA = 1.702
SWIGLU_LIMIT = 7.0
NEG = -1e30
LANES = 128
MIB = 2 ** 20


def _params(n_axes, vmem_mib):
    return pltpu.CompilerParams(dimension_semantics=("arbitrary",) * n_axes,
                                vmem_limit_bytes=vmem_mib * MIB)


def _inproj_body(x_ref, g_ref, w_ref, wg_ref, o_ref, gate_ref, h_ref):
    @pl.when(pl.program_id(1) == 0)
    def _():
        xf = x_ref[...]
        ms = jnp.mean(xf * xf, axis=-1, keepdims=True)
        h = (xf * lax.rsqrt(ms + NORM_EPS) * g_ref[...]).astype(BF16)
        h_ref[...] = h
        gate_ref[...] = jnp.dot(h, wg_ref[...], preferred_element_type=F32)

    o_ref[...] = jnp.dot(h_ref[...], w_ref[...], preferred_element_type=F32)


def _inproj(x2d, g, w_main, w_gate, tm, tn):
    n, d = x2d.shape
    wcols = w_main.shape[1]
    return pl.pallas_call(
        _inproj_body,
        grid=(n // tm, wcols // tn),
        in_specs=[pl.BlockSpec((tm, d), lambda i, j: (i, 0)),
                  pl.BlockSpec((1, d), lambda i, j: (0, 0)),
                  pl.BlockSpec((d, tn), lambda i, j: (0, j)),
                  pl.BlockSpec((d, LANES), lambda i, j: (0, 0))],
        out_specs=[pl.BlockSpec((tm, tn), lambda i, j: (i, j)),
                   pl.BlockSpec((tm, LANES), lambda i, j: (i, 0))],
        out_shape=[jax.ShapeDtypeStruct((n, wcols), F32),
                   jax.ShapeDtypeStruct((n, LANES), F32)],
        scratch_shapes=[pltpu.VMEM((tm, d), BF16)],
        compiler_params=_params(2, 48),
        name="inproj",
    )(x2d, g, w_main, w_gate)


def _attn_body(q_ref, kc_ref, kp_ref, vc_ref, vp_ref, gq_ref, gk_ref, tc_ref, tp_ref, o_ref,
               qb, kb, vb, acc_s, m_s, l_s):
    t = pl.program_id(1)
    T = ATT_TILE
    CH = 256
    scale = 1.0 / math.sqrt(HEAD_DIM)

    def norm_rope(x, g, tab):
        ms = jnp.mean(x * x, axis=-1, keepdims=True)
        y = x * lax.rsqrt(ms + NORM_EPS) * g
        c = tab[:, 0:LANES]
        s1 = tab[:, LANES:2 * LANES]
        s2 = tab[:, 2 * LANES:3 * LANES]
        return y * c + pltpu.roll(y, LANES - ROPE_DIM // 2, 1) * s1 + pltpu.roll(y, ROPE_DIM // 2, 1) * s2

    def prep(c, carry):
        rows = pl.ds(pl.multiple_of(c * CH, CH), CH)
        prow = pl.ds(pl.multiple_of(c * CH + T, CH), CH)
        tabc = tc_ref[rows, :]
        qb[rows, :] = norm_rope(q_ref[rows, :], gq_ref[...], tabc) * scale
        kb[prow, :] = norm_rope(kc_ref[rows, :], gk_ref[...], tabc)
        kb[rows, :] = norm_rope(kp_ref[rows, :], gk_ref[...], tp_ref[rows, :])
        vb[rows, :] = vp_ref[rows, :]
        vb[prow, :] = vc_ref[rows, :]
        return carry

    lax.fori_loop(0, T // CH, prep, 0)

    def unit(bidx, d, u):
        seg = BAND * d
        if d == 1:
            sg, qoff = u, pl.multiple_of(u * BAND, BAND)
        elif d == DILATIONS[-1]:
            sg, qoff = 0, u
        else:
            sg = lax.shift_right_logical(u, 2)
            qoff = sg * seg + lax.bitwise_and(u, d - 1)

        def rows(off):
            return pl.ds(off, BAND) if d == 1 else pl.ds(off, BAND, stride=d)

        qs = qb[rows(qoff), :].astype(BF16)
        kk = jnp.concatenate([kb[rows(T + qoff - seg), :], kb[rows(T + qoff), :]], axis=0).astype(BF16)
        vv = jnp.concatenate([vb[rows(T + qoff - seg), :], vb[rows(T + qoff), :]], axis=0).astype(BF16)
        s = lax.dot_general(qs, kk, (((1,), (1,)), ((), ())), preferred_element_type=F32)
        ii = lax.broadcasted_iota(I32, (BAND, 2 * BAND), 0)
        jj = lax.broadcasted_iota(I32, (BAND, 2 * BAND), 1)
        dist = BAND + ii - jj
        first = jnp.logical_and(t == 0, sg == 0)
        jmin = jnp.where(first, BAND, 0)
        valid = (dist >= 0) & (dist <= BAND) & (jj >= jmin)
        s = jnp.where(valid, s, NEG)
        m = jnp.max(s, axis=-1, keepdims=True)
        p = jnp.exp(s - m)
        l = jnp.sum(p, axis=-1, keepdims=True)
        acc = jnp.dot(p.astype(BF16), vv, preferred_element_type=F32)
        acc_s.at[bidx][rows(qoff), :] = acc
        m_s.at[bidx][rows(qoff), :] = jnp.broadcast_to(m, (BAND, LANES))
        l_s.at[bidx][rows(qoff), :] = jnp.broadcast_to(l, (BAND, LANES))

    for bidx, d in enumerate(DILATIONS):
        def body(u, carry, bidx=bidx, d=d):
            unit(bidx, d, u)
            return carry
        lax.fori_loop(0, T // BAND, body, 0)

    def mix(c, carry):
        rows = pl.ds(pl.multiple_of(c * CH, CH), CH)
        ms = [m_s[b, rows, :] for b in range(3)]
        mx = jnp.maximum(jnp.maximum(ms[0], ms[1]), ms[2])
        num = jnp.zeros((CH, LANES), F32)
        den = jnp.zeros((CH, LANES), F32)
        for b in range(3):
            w = jnp.exp(ms[b] - mx)
            num = num + w * acc_s[b, rows, :]
            den = den + w * l_s[b, rows, :]
        o_ref[rows, :] = (num / den).astype(o_ref.dtype)
        return carry

    lax.fori_loop(0, T // CH, mix, 0)


def _attention(proj, gq, gk, tab, batch, seq):
    n = proj.shape[0]
    T = ATT_TILE
    nt = seq // T
    H = ATT_HEADS

    def cur(col0):
        return pl.BlockSpec((T, HEAD_DIM), lambda b, t, h: (b * nt + t, col0 + h))

    def prev(col0):
        return pl.BlockSpec((T, HEAD_DIM), lambda b, t, h: (b * nt + jnp.maximum(t - 1, 0), col0 + h))

    return pl.pallas_call(
        _attn_body,
        grid=(batch, nt, H),
        in_specs=[cur(0), cur(H), prev(H), cur(2 * H), prev(2 * H),
                  pl.BlockSpec((1, HEAD_DIM), lambda b, t, h: (0, 0)),
                  pl.BlockSpec((1, HEAD_DIM), lambda b, t, h: (0, 0)),
                  pl.BlockSpec((T, 3 * LANES), lambda b, t, h: (t, 0)),
                  pl.BlockSpec((T, 3 * LANES), lambda b, t, h: (jnp.maximum(t - 1, 0), 0))],
        out_specs=pl.BlockSpec((T, HEAD_DIM), lambda b, t, h: (b * nt + t, h)),
        out_shape=jax.ShapeDtypeStruct((n, H * HEAD_DIM), BF16),
        scratch_shapes=[pltpu.VMEM((T, HEAD_DIM), F32),
                        pltpu.VMEM((2 * T, HEAD_DIM), F32),
                        pltpu.VMEM((2 * T, HEAD_DIM), F32),
                        pltpu.VMEM((3, T, HEAD_DIM), F32),
                        pltpu.VMEM((3, T, LANES), F32),
                        pltpu.VMEM((3, T, LANES), F32)],
        compiler_params=_params(3, 56),
        name="dilated_attn",
    )(proj, proj, proj, proj, proj, gq, gk, tab, tab)


def _rope_table(seq):
    half = ROPE_DIM // 2
    inv = ROPE_THETA ** (-jnp.arange(half, dtype=F32) * 2.0 / ROPE_DIM)
    ang = jnp.arange(seq, dtype=F32)[:, None] * inv[None, :]
    cos, sin = jnp.cos(ang), jnp.sin(ang)
    z = lambda w: jnp.zeros((seq, w), F32)
    c = jnp.concatenate([cos, cos, jnp.ones((seq, LANES - ROPE_DIM), F32)], axis=1)
    s1 = jnp.concatenate([-sin, z(LANES - half)], axis=1)
    s2 = jnp.concatenate([z(half), sin, z(LANES - ROPE_DIM)], axis=1)
    return jnp.concatenate([c, s1, s2], axis=1)


def _mlstm_body(mq_ref, mk_ref, mv_ref, mo_ref, gt_ref, gb_ref, cw_ref, cb_ref, og_ref, o_ref,
                prev_s, c_s, n_s, m_s):
    L = ML_CHUNK
    QW = ML_HEADS * ML_QK

    @pl.when(pl.program_id(1) == 0)
    def _():
        prev_s[...] = jnp.zeros_like(prev_s)
        c_s[...] = jnp.zeros_like(c_s)
        n_s[...] = jnp.zeros_like(n_s)
        m_s[...] = jnp.zeros_like(m_s)

    cur = jnp.concatenate([mq_ref[...], mk_ref[...]], axis=1)
    prv = prev_s[...]
    rid = lax.broadcasted_iota(I32, (L, 2 * QW), 0)
    conv = cur * cw_ref[ML_CONV - 1:ML_CONV, :] + cb_ref[...]
    for s in range(1, ML_CONV):
        shifted = jnp.where(rid < s, pltpu.roll(prv, s, 0), pltpu.roll(cur, s, 0))
        conv = conv + shifted * cw_ref[ML_CONV - 1 - s:ML_CONV - s, :]
    prev_s[...] = cur
    qk = conv * jax.nn.sigmoid(conv)

    gates = gt_ref[...] + gb_ref[...]
    logf = jnp.minimum(gates, 0.0) - jnp.log(1.0 + jnp.exp(-jnp.abs(gates)))
    ti = lax.broadcasted_iota(I32, (L, L), 0)
    si = lax.broadcasted_iota(I32, (L, L), 1)
    causal = si <= ti
    tril = jnp.where(causal, 1.0, 0.0).astype(F32)
    bmat = jnp.dot(tril, logf, precision=lax.Precision.HIGHEST, preferred_element_type=F32)
    rmat = gates - pltpu.roll(bmat, LANES - ML_HEADS, 1)
    rmat_t = rmat.T

    for h in range(ML_HEADS):
        q = qk[:, h * ML_QK:(h + 1) * ML_QK]
        k = qk[:, QW + h * ML_QK:QW + (h + 1) * ML_QK] * (1.0 / math.sqrt(ML_QK))
        v = mv_ref[:, h * ML_V:(h + 1) * ML_V]
        qb, kb, vb = q.astype(BF16), k.astype(BF16), v.astype(BF16)
        b_col = bmat[:, ML_HEADS + h:ML_HEADS + h + 1]
        r_col = rmat[:, h:h + 1]
        r_row = rmat_t[h:h + 1, :]
        g_tot = b_col[L - 1:L, :]
        m_prev = m_s[h]
        c_prev = c_s[h]
        n_prev = n_s[h]

        dmat = jnp.where(causal, b_col + r_row, NEG)
        inter = b_col + m_prev
        m_t = jnp.maximum(inter, jnp.max(dmat, axis=-1, keepdims=True))
        qkt = lax.dot_general(qb, kb, (((1,), (1,)), ((), ())), preferred_element_type=F32)
        pw = jnp.exp(dmat - m_t) * qkt
        wi = jnp.exp(inter - m_t)
        num = wi * jnp.dot(qb, c_prev.astype(BF16), preferred_element_type=F32) \
            + jnp.dot(pw.astype(BF16), vb, preferred_element_type=F32)
        den = wi * jnp.sum(q * n_prev, axis=-1, keepdims=True) + jnp.sum(pw, axis=-1, keepdims=True)
        hh = num / jnp.maximum(jnp.abs(den), jnp.exp(-m_t))
        ms = jnp.mean(hh * hh, axis=-1, keepdims=True)
        hn = hh * lax.rsqrt(ms + NORM_EPS) * og_ref[:, h * ML_V:(h + 1) * ML_V]
        hn = hn * jax.nn.sigmoid(mo_ref[:, h * ML_V:(h + 1) * ML_V])
        o_ref[:, h * ML_V:(h + 1) * ML_V] = hn.astype(o_ref.dtype)

        a_col = g_tot + r_col
        m_loc = jnp.max(a_col, axis=0, keepdims=True)
        wk = jnp.exp(a_col - m_loc)
        wkk = wk * k
        c_loc = lax.dot_general(wkk.astype(BF16), vb, (((0,), (0,)), ((), ())), preferred_element_type=F32)
        n_loc = jnp.sum(wkk, axis=0, keepdims=True)
        m_new = jnp.maximum(g_tot + m_prev, m_loc)
        d_old = jnp.exp(g_tot + m_prev - m_new)
        d_new = jnp.exp(m_loc - m_new)
        c_s[h] = d_old * c_prev + d_new * c_loc
        n_s[h] = d_old * n_prev + d_new * n_loc
        m_s[h] = m_new


def _mlstm(proj, gates, gate_bias, conv_w, conv_b, out_g, batch, seq):
    n = proj.shape[0]
    L = ML_CHUNK
    nc = seq // L
    QW = ML_HEADS * ML_QK
    VW = ML_HEADS * ML_V
    q_col = 3 * ATT_HEADS * HEAD_DIM
    assert q_col % QW == 0 and (q_col + 2 * QW) % VW == 0
    qi = q_col // QW
    vi = (q_col + 2 * QW) // VW
    row = lambda b, c: b * nc + c
    return pl.pallas_call(
        _mlstm_body,
        grid=(batch, nc),
        in_specs=[pl.BlockSpec((L, QW), lambda b, c: (row(b, c), qi)),
                  pl.BlockSpec((L, QW), lambda b, c: (row(b, c), qi + 1)),
                  pl.BlockSpec((L, VW), lambda b, c: (row(b, c), vi)),
                  pl.BlockSpec((L, VW), lambda b, c: (row(b, c), vi + 1)),
                  pl.BlockSpec((L, LANES), lambda b, c: (row(b, c), 0)),
                  pl.BlockSpec((1, LANES), lambda b, c: (0, 0)),
                  pl.BlockSpec((ML_CONV, 2 * QW), lambda b, c: (0, 0)),
                  pl.BlockSpec((1, 2 * QW), lambda b, c: (0, 0)),
                  pl.BlockSpec((1, VW), lambda b, c: (0, 0))],
        out_specs=pl.BlockSpec((L, VW), lambda b, c: (row(b, c), 0)),
        out_shape=jax.ShapeDtypeStruct((n, VW), BF16),
        scratch_shapes=[pltpu.VMEM((L, 2 * QW), F32),
                        pltpu.VMEM((ML_HEADS, ML_QK, ML_V), F32),
                        pltpu.VMEM((ML_HEADS, 1, ML_QK), F32),
                        pltpu.VMEM((ML_HEADS, 1, 1), F32)],
        compiler_params=_params(2, 32),
        name="mlstm",
    )(proj, proj, proj, proj, gates, gate_bias, conv_w, conv_b, out_g)


def _outproj_body(att_ref, hm_ref, x_ref, wa_ref, wm_ref, g_ref, rw_ref, rb_ref,
                  x1_ref, h2_ref, idx_ref, gate_ref, rank_ref, cnt_ref, carry):
    tm = x_ref.shape[0]

    @pl.when(pl.program_id(0) == 0)
    def _():
        carry[...] = jnp.zeros_like(carry)

    y = jnp.dot(att_ref[...], wa_ref[...], preferred_element_type=F32)
    y = y + jnp.dot(hm_ref[...], wm_ref[...], preferred_element_type=F32)
    x1 = x_ref[...] + y
    x1_ref[...] = x1
    ms = jnp.mean(x1 * x1, axis=-1, keepdims=True)
    h2 = x1 * lax.rsqrt(ms + NORM_EPS) * g_ref[...]
    h2_ref[...] = h2
    logits = jnp.dot(h2, rw_ref[...], precision=lax.Precision.HIGHEST,
                     preferred_element_type=F32) + rb_ref[...]
    lane = lax.broadcasted_iota(I32, (tm, LANES), 1)
    work = jnp.where(lane < N_EXPERTS, logits, NEG)
    vals, idxs = [], []
    onehot = jnp.zeros((tm, LANES), F32)
    for _ in range(TOP_K):
        mx = jnp.max(work, axis=-1, keepdims=True)
        ix = jnp.min(jnp.where(work == mx, lane, LANES), axis=-1, keepdims=True)
        sel = lane == ix
        onehot = jnp.where(sel, 1.0, onehot)
        work = jnp.where(sel, 2.0 * NEG, work)
        vals.append(mx)
        idxs.append(ix)
    es = [jnp.exp(v - vals[0]) for v in vals]
    tot = es[0] + es[1] + es[2] + es[3]

    ri = lax.broadcasted_iota(I32, (tm, tm), 0)
    ci = lax.broadcasted_iota(I32, (tm, tm), 1)
    strict = jnp.where(ci < ri, 1.0, 0.0).astype(BF16)
    cum = jnp.dot(strict, onehot.astype(BF16), preferred_element_type=F32) + carry[0:1, :]
    idx_out = jnp.zeros((tm, LANES), I32)
    gate_out = jnp.zeros((tm, LANES), F32)
    rank_out = jnp.zeros((tm, LANES), I32)
    for k in range(TOP_K):
        rk = jnp.sum(jnp.where(lane == idxs[k], cum, 0.0), axis=-1, keepdims=True)
        idx_out = jnp.where(lane == k, idxs[k], idx_out)
        gate_out = jnp.where(lane == k, es[k] / tot, gate_out)
        rank_out = jnp.where(lane == k, rk.astype(I32), rank_out)
    idx_ref[...] = idx_out
    gate_ref[...] = gate_out
    rank_ref[...] = rank_out
    new = carry[...] + jnp.sum(onehot, axis=0, keepdims=True)
    carry[...] = new
    cnt_ref[...] = new


def _outproj(att, hm, x2d, wa, wm, g, rw, rb, tm):
    n, d = x2d.shape
    row = lambda i: (i, 0)
    const = lambda i: (0, 0)
    return pl.pallas_call(
        _outproj_body,
        grid=(n // tm,),
        in_specs=[pl.BlockSpec((tm, att.shape[1]), row),
                  pl.BlockSpec((tm, hm.shape[1]), row),
                  pl.BlockSpec((tm, d), row),
                  pl.BlockSpec(wa.shape, const),
                  pl.BlockSpec(wm.shape, const),
                  pl.BlockSpec((1, d), const),
                  pl.BlockSpec((d, LANES), const),
                  pl.BlockSpec((1, LANES), const)],
        out_specs=[pl.BlockSpec((tm, d), row),
                   pl.BlockSpec((tm, d), row),
                   pl.BlockSpec((tm, LANES), row),
                   pl.BlockSpec((tm, LANES), row),
                   pl.BlockSpec((tm, LANES), row),
                   pl.BlockSpec((8, LANES), const)],
        out_shape=[jax.ShapeDtypeStruct((n, d), F32),
                   jax.ShapeDtypeStruct((n, d), F32),
                   jax.ShapeDtypeStruct((n, LANES), I32),
                   jax.ShapeDtypeStruct((n, LANES), F32),
                   jax.ShapeDtypeStruct((n, LANES), I32),
                   jax.ShapeDtypeStruct((8, LANES), F32)],
        scratch_shapes=[pltpu.VMEM((8, LANES), F32)],
        compiler_params=_params(1, 56),
        name="outproj_router",
    )(att, hm, x2d, wa, wm, g, rw, rb)


def _split_glu_body(w_ref, g_ref, l_ref):
    grp = 2 * LANES
    r = lax.broadcasted_iota(I32, (grp, grp), 0)
    c = lax.broadcasted_iota(I32, (grp, grp), 1)
    src = jnp.where(c < LANES, 2 * c, 2 * (c - LANES) + 1)
    perm = jnp.where(r == src, 1.0, 0.0).astype(BF16)
    for j in range(w_ref.shape[2] // grp):
        blk = w_ref[0, :, j * grp:(j + 1) * grp].astype(BF16)
        out = jnp.dot(blk, perm, preferred_element_type=F32).astype(BF16)
        g_ref[0, :, j * LANES:(j + 1) * LANES] = out[:, :LANES]
        l_ref[0, :, j * LANES:(j + 1) * LANES] = out[:, LANES:]


def _split_glu(w_up, tr, tc):
    e, d, f2 = w_up.shape
    out = jax.ShapeDtypeStruct((e, d, f2 // 2), BF16)
    return pl.pallas_call(
        _split_glu_body,
        grid=(e, d // tr, f2 // tc),
        in_specs=[pl.BlockSpec((1, tr, tc), lambda a, i, j: (a, i, j))],
        out_specs=[pl.BlockSpec((1, tr, tc // 2), lambda a, i, j: (a, i, j)),
                   pl.BlockSpec((1, tr, tc // 2), lambda a, i, j: (a, i, j))],
        out_shape=[out, out],
        compiler_params=_params(3, 48),
        name="split_glu_weights",
    )(w_up)


def _moe_body(te_ref, tv_ref, to_ref, rowtok_hbm, h2_hbm, wg_ref, wl_ref, bg_ref, bl_ref, wd_ref, bd_ref,
              ys_ref, idx_smem, xf_buf, xb_buf, sem_idx, sem_rows):
    i = pl.program_id(0)
    f = pl.program_id(1)
    tm = xf_buf.shape[0]
    valid = tv_ref[i] > 0

    @pl.when(jnp.logical_and(valid, f == 0))
    def _():
        cp = pltpu.make_async_copy(rowtok_hbm.at[i], idx_smem, sem_idx)
        cp.start()
        cp.wait()

        def issue(r, carry):
            tok = idx_smem[r]
            pltpu.make_async_copy(h2_hbm.at[pl.ds(tok, 1), :], xf_buf.at[pl.ds(r, 1), :], sem_rows).start()
            return carry

        lax.fori_loop(0, tm, issue, 0, unroll=8)
        pltpu.make_async_copy(h2_hbm.at[pl.ds(0, tm), :], xf_buf, sem_rows).wait()
        xb_buf[...] = xf_buf[...].astype(BF16)

    @pl.when(valid)
    def _():
        x = xb_buf[...]
        hg = jnp.dot(x, wg_ref[0], preferred_element_type=F32) + bg_ref[0]
        hl = jnp.dot(x, wl_ref[0], preferred_element_type=F32) + bl_ref[0]
        glu = jnp.minimum(hg, SWIGLU_LIMIT)
        lin = jnp.clip(hl, -SWIGLU_LIMIT, SWIGLU_LIMIT)
        act = glu * jax.nn.sigmoid(SWIGLU_ALPHA * glu) * (lin + 1.0)
        y = jnp.dot(act.astype(BF16), wd_ref[0], preferred_element_type=F32)

        @pl.when(f == 0)
        def _():
            ys_ref[...] = y + bd_ref[0]

        @pl.when(f > 0)
        def _():
            ys_ref[...] += y


def _moe(tile_e, tile_valid, tile_out, row_tok, h2, wg, wl, bg, bl, wd, bd, tm, fc):
    nt = row_tok.shape[0]
    n, d = h2.shape
    ff = wg.shape[2]
    nf = ff // fc

    def fidx(i, f, tv):
        return jnp.where(tv[i] > 0, f, nf - 1)

    grid_spec = pltpu.PrefetchScalarGridSpec(
        num_scalar_prefetch=3,
        grid=(nt, nf),
        in_specs=[pl.BlockSpec(memory_space=pl.ANY),
                  pl.BlockSpec(memory_space=pl.ANY),
                  pl.BlockSpec((1, d, fc), lambda i, f, te, tv, to: (te[i], 0, fidx(i, f, tv))),
                  pl.BlockSpec((1, d, fc), lambda i, f, te, tv, to: (te[i], 0, fidx(i, f, tv))),
                  pl.BlockSpec((1, 1, fc), lambda i, f, te, tv, to: (te[i], 0, fidx(i, f, tv))),
                  pl.BlockSpec((1, 1, fc), lambda i, f, te, tv, to: (te[i], 0, fidx(i, f, tv))),
                  pl.BlockSpec((1, fc, d), lambda i, f, te, tv, to: (te[i], fidx(i, f, tv), 0)),
                  pl.BlockSpec((1, 1, d), lambda i, f, te, tv, to: (te[i], 0, 0))],
        out_specs=pl.BlockSpec((tm, d), lambda i, f, te, tv, to: (to[i], 0)),
        scratch_shapes=[pltpu.SMEM((tm,), I32),
                        pltpu.VMEM((tm, d), F32),
                        pltpu.VMEM((tm, d), BF16),
                        pltpu.SemaphoreType.DMA,
                        pltpu.SemaphoreType.DMA],
    )
    return pl.pallas_call(
        _moe_body,
        grid_spec=grid_spec,
        out_shape=jax.ShapeDtypeStruct((nt * tm, d), F32),
        compiler_params=_params(2, 56),
        name="moe_ffn",
    )(tile_e, tile_valid, tile_out, row_tok, h2, wg, wl, bg, bl, wd, bd)


def _combine_body(dest_hbm, ys_hbm, x1_ref, gate_ref, o_ref, idx_smem, buf, sem_idx, sem_rows):
    i = pl.program_id(0)
    tm = x1_ref.shape[0]
    cp = pltpu.make_async_copy(dest_hbm.at[i], idx_smem, sem_idx)
    cp.start()
    cp.wait()
    for k in range(TOP_K):
        def issue(r, carry, k=k):
            row = idx_smem[k * tm + r]
            pltpu.make_async_copy(ys_hbm.at[pl.ds(row, 1), :], buf.at[k, pl.ds(r, 1), :], sem_rows).start()
            return carry
        lax.fori_loop(0, tm, issue, 0, unroll=8)
    for k in range(TOP_K):
        pltpu.make_async_copy(ys_hbm.at[pl.ds(0, tm), :], buf.at[k], sem_rows).wait()
    out = x1_ref[...]
    g = gate_ref[...]
    for k in range(TOP_K):
        out = out + g[:, k:k + 1] * buf[k]
    o_ref[...] = out


def _combine(dest_tiles, ys, x1, gates, tm):
    n, d = x1.shape
    return pl.pallas_call(
        _combine_body,
        grid=(n // tm,),
        in_specs=[pl.BlockSpec(memory_space=pl.ANY),
                  pl.BlockSpec(memory_space=pl.ANY),
                  pl.BlockSpec((tm, d), lambda i: (i, 0)),
                  pl.BlockSpec((tm, LANES), lambda i: (i, 0))],
        out_specs=pl.BlockSpec((tm, d), lambda i: (i, 0)),
        out_shape=jax.ShapeDtypeStruct((n, d), F32),
        scratch_shapes=[pltpu.SMEM((TOP_K * tm,), I32),
                        pltpu.VMEM((TOP_K, tm, d), F32),
                        pltpu.SemaphoreType.DMA,
                        pltpu.SemaphoreType.DMA],
        compiler_params=_params(1, 40),
        name="moe_combine",
    )(dest_tiles, ys, x1, gates)


def _pad_lanes(a, width=LANES):
    return jnp.pad(a, [(0, 0)] * (a.ndim - 1) + [(0, width - a.shape[-1])])


def kernel(x, attn_norm_g, w_in, q_norm_g, k_norm_g, ml_conv_w, ml_conv_b, ml_i_b, ml_f_b, ml_out_norm_g,
           w_out, ffn_norm_g, router_w, router_b, w_up, b_up, w_down, b_down):
    B, S, D = x.shape
    N = B * S
    assert S % ATT_TILE == 0 and attn_norm_g.shape[0] == 1
    att_w = ATT_HEADS * HEAD_DIM
    main_w = 3 * att_w + 2 * ML_HEADS * ML_QK + 2 * ML_HEADS * ML_V
    x2d = x.reshape(N, D)

    w_in0 = w_in[0]
    w_main = w_in0[:, :main_w].astype(BF16)
    w_gate = _pad_lanes(w_in0[:, main_w:]).astype(BF16)
    proj, gates = _inproj(x2d, attn_norm_g[0][None, :], w_main, w_gate, tm=1024, tn=1024)

    att = _attention(proj, q_norm_g[0][None, :], k_norm_g[0][None, :], _rope_table(S), B, S)

    gate_bias = _pad_lanes(jnp.concatenate([ml_i_b[0], ml_f_b[0]])[None, :]).astype(F32)
    hm = _mlstm(proj, gates, gate_bias, ml_conv_w[0], ml_conv_b[0][None, :], ml_out_norm_g[0][None, :], B, S)

    w_out0 = w_out[0].astype(BF16)
    x1, h2, top_idx, top_gate, top_rank, counts = _outproj(
        att, hm, x2d, w_out0[:att_w], w_out0[att_w:], ffn_norm_g[0][None, :],
        _pad_lanes(router_w[0]).astype(F32), _pad_lanes(router_b[0][None, :]).astype(F32), tm=512)

    tm_moe = 512
    n_tiles = N * TOP_K // tm_moe + N_EXPERTS
    cnt = counts[0, :N_EXPERTS].astype(I32)
    ptiles = (cnt + tm_moe - 1) // tm_moe
    tend = jnp.cumsum(ptiles)
    total = tend[-1]
    pstart = (tend - ptiles) * tm_moe
    e_idx = top_idx[:, :TOP_K]
    dest = pstart[e_idx] + top_rank[:, :TOP_K]
    tok = jnp.broadcast_to(jnp.arange(N, dtype=I32)[:, None], (N, TOP_K))
    row_tok = jnp.zeros((n_tiles * tm_moe,), I32).at[dest.reshape(-1)].set(tok.reshape(-1))
    tile_ids = jnp.arange(n_tiles, dtype=I32)
    tile_e = jnp.minimum(jnp.searchsorted(tend, tile_ids, side="right"), N_EXPERTS - 1).astype(I32)
    tile_valid = (tile_ids < total).astype(I32)
    tile_out = jnp.minimum(tile_ids, total - 1).astype(I32)

    wg, wl = _split_glu(w_up[0], tr=1024, tc=2048)
    bg = b_up[0][:, None, 0::2].astype(F32)
    bl = b_up[0][:, None, 1::2].astype(F32)
    wd = w_down[0].astype(BF16)
    bd = b_down[0][:, None, :].astype(F32)
    ys = _moe(tile_e, tile_valid, tile_out, row_tok.reshape(n_tiles, tm_moe), h2, wg, wl, bg, bl, wd, bd,
              tm=tm_moe, fc=1024)

    tm_c = 256
    dest_tiles = dest.reshape(N // tm_c, tm_c, TOP_K).transpose(0, 2, 1).reshape(N // tm_c, TOP_K * tm_c)
    out = _combine(dest_tiles, ys, x1, top_gate, tm=tm_c)
    return out.reshape(B, S, D)
```

```python
import functools
import math

import jax
import jax.numpy as jnp
from jax import lax
from jax.experimental import pallas as pl
from jax.experimental.pallas import tpu as pltpu

F32 = jnp.float32
BF16 = jnp.bfloat16
I32 = jnp.int32

NORM_EPS = 1e-6
ATT_HEADS = 8
HEAD_DIM = 128
ROPE_DIM = 32
ROPE_THETA = 500000.0
DILATIONS = (1, 4, 16)
BAND = 128
ATT_TILE = 2048
ML_HEADS = 4
ML_QK = 128
ML_V = 256
ML_CHUNK = 128
ML_CONV = 4
N_EXPERTS = 32
TOP_K = 4
SWIGLU_ALPHA = 1.702
SWIGLU_LIMIT = 7.0
NEG = -1e30
LANES = 128
MIB = 2 ** 20


def _params(n_axes, vmem_mib):
    return pltpu.CompilerParams(dimension_semantics=("arbitrary",) * n_axes,
                                vmem_limit_bytes=vmem_mib * MIB)


def _inproj_body(x_ref, g_ref, w_ref, wg_ref, o_ref, gate_ref, h_ref):
    @pl.when(pl.program_id(1) == 0)
    def _():
        xf = x_ref[...]
        ms = jnp.mean(xf * xf, axis=-1, keepdims=True)
        h = (xf * lax.rsqrt(ms + NORM_EPS) * g_ref[...]).astype(BF16)
        h_ref[...] = h
        gate_ref[...] = jnp.dot(h, wg_ref[...], preferred_element_type=F32)

    o_ref[...] = jnp.dot(h_ref[...], w_ref[...], preferred_element_type=F32)


def _inproj(x2d, g, w_main, w_gate, tm, tn):
    n, d = x2d.shape
    wcols = w_main.shape[1]
    return pl.pallas_call(
        _inproj_body,
        grid=(n // tm, wcols // tn),
        in_specs=[pl.BlockSpec((tm, d), lambda i, j: (i, 0)),
                  pl.BlockSpec((1, d), lambda i, j: (0, 0)),
                  pl.BlockSpec((d, tn), lambda i, j: (0, j)),
                  pl.BlockSpec((d, LANES), lambda i, j: (0, 0))],
        out_specs=[pl.BlockSpec((tm, tn), lambda i, j: (i, j)),
                   pl.BlockSpec((tm, LANES), lambda i, j: (i, 0))],
        out_shape=[jax.ShapeDtypeStruct((n, wcols), F32),
                   jax.ShapeDtypeStruct((n, LANES), F32)],
        scratch_shapes=[pltpu.VMEM((tm, d), BF16)],
        compiler_params=_params(2, 48),
        name="inproj",
    )(x2d, g, w_main, w_gate)


def _attn_body(q_ref, kc_ref, kp_ref, vc_ref, vp_ref, gq_ref, gk_ref, tc_ref, tp_ref, o_ref,
               qb, kb, vb, acc_s, m_s, l_s):
    t = pl.program_id(1)
    T = ATT_TILE
    CH = 256
    scale = 1.0 / math.sqrt(HEAD_DIM)

    ones = jnp.ones((HEAD_DIM, LANES), BF16)

    def norm_rope(x, g, tab):
        sq = x * x
        hi = sq.astype(BF16)
        lo = (sq - hi.astype(F32)).astype(BF16)
        ssq = jnp.dot(hi, ones, preferred_element_type=F32) + jnp.dot(lo, ones, preferred_element_type=F32)
        y = x * lax.rsqrt(ssq * (1.0 / HEAD_DIM) + NORM_EPS) * g
        c = tab[:, 0:LANES]
        s1 = tab[:, LANES:2 * LANES]
        s2 = tab[:, 2 * LANES:3 * LANES]
        return y * c + pltpu.roll(y, LANES - ROPE_DIM // 2, 1) * s1 + pltpu.roll(y, ROPE_DIM // 2, 1) * s2

    def prep(c, carry):
        rows = pl.ds(pl.multiple_of(c * CH, CH), CH)
        prow = pl.ds(pl.multiple_of(c * CH + T, CH), CH)
        tabc = tc_ref[rows, :]
        qb[rows, :] = norm_rope(q_ref[rows, :], gq_ref[...], tabc) * scale
        kb[prow, :] = norm_rope(kc_ref[rows, :], gk_ref[...], tabc)
        kb[rows, :] = norm_rope(kp_ref[rows, :], gk_ref[...], tp_ref[rows, :])
        vb[rows, :] = vp_ref[rows, :]
        vb[prow, :] = vc_ref[rows, :]
        return carry

    lax.fori_loop(0, T // CH, prep, 0)

    def unit(bidx, d, u):
        seg = BAND * d
        if d == 1:
            sg, qoff = u, pl.multiple_of(u * BAND, BAND)
        elif d == DILATIONS[-1]:
            sg, qoff = 0, u
        else:
            sg = lax.shift_right_logical(u, 2)
            qoff = sg * seg + lax.bitwise_and(u, d - 1)

        def rows(off):
            return pl.ds(off, BAND) if d == 1 else pl.ds(off, BAND, stride=d)

        qs = qb[rows(qoff), :].astype(BF16)
        kk = jnp.concatenate([kb[rows(T + qoff - seg), :], kb[rows(T + qoff), :]], axis=0).astype(BF16)
        vv = jnp.concatenate([vb[rows(T + qoff - seg), :], vb[rows(T + qoff), :]], axis=0).astype(BF16)
        s = lax.dot_general(qs, kk, (((1,), (1,)), ((), ())), preferred_element_type=F32)
        ii = lax.broadcasted_iota(I32, (BAND, 2 * BAND), 0)
        jj = lax.broadcasted_iota(I32, (BAND, 2 * BAND), 1)
        dist = BAND + ii - jj
        first = jnp.logical_and(t == 0, sg == 0)
        jmin = jnp.where(first, BAND, 0)
        valid = (dist >= 0) & (dist <= BAND) & (jj >= jmin)
        s = jnp.where(valid, s, NEG)
        m = jnp.max(s, axis=-1, keepdims=True)
        p = jnp.exp(s - m)
        l = jnp.sum(p, axis=-1, keepdims=True)
        acc = jnp.dot(p.astype(BF16), vv, preferred_element_type=F32)
        acc_s.at[bidx][rows(qoff), :] = acc
        m_s.at[bidx][rows(qoff), :] = jnp.broadcast_to(m, (BAND, LANES))
        l_s.at[bidx][rows(qoff), :] = jnp.broadcast_to(l, (BAND, LANES))

    for bidx, d in enumerate(DILATIONS):
        def body(u, carry, bidx=bidx, d=d):
            unit(bidx, d, u)
            return carry
        lax.fori_loop(0, T // BAND, body, 0, unroll=4)

    def mix(c, carry):
        rows = pl.ds(pl.multiple_of(c * CH, CH), CH)
        ms = [m_s[b, rows, :] for b in range(3)]
        mx = jnp.maximum(jnp.maximum(ms[0], ms[1]), ms[2])
        num = jnp.zeros((CH, LANES), F32)
        den = jnp.zeros((CH, LANES), F32)
        for b in range(3):
            w = jnp.exp(ms[b] - mx)
            num = num + w * acc_s[b, rows, :]
            den = den + w * l_s[b, rows, :]
        o_ref[rows, :] = (num / den).astype(o_ref.dtype)
        return carry

    lax.fori_loop(0, T // CH, mix, 0)


def _attention(proj, gq, gk, tab, batch, seq):
    n = proj.shape[0]
    T = ATT_TILE
    nt = seq // T
    H = ATT_HEADS

    def cur(col0):
        return pl.BlockSpec((T, HEAD_DIM), lambda b, t, h: (b * nt + t, col0 + h))

    def prev(col0):
        return pl.BlockSpec((T, HEAD_DIM), lambda b, t, h: (b * nt + jnp.maximum(t - 1, 0), col0 + h))

    return pl.pallas_call(
        _attn_body,
        grid=(batch, nt, H),
        in_specs=[cur(0), cur(H), prev(H), cur(2 * H), prev(2 * H),
                  pl.BlockSpec((1, HEAD_DIM), lambda b, t, h: (0, 0)),
                  pl.BlockSpec((1, HEAD_DIM), lambda b, t, h: (0, 0)),
                  pl.BlockSpec((T, 3 * LANES), lambda b, t, h: (t, 0)),
                  pl.BlockSpec((T, 3 * LANES), lambda b, t, h: (jnp.maximum(t - 1, 0), 0))],
        out_specs=pl.BlockSpec((T, HEAD_DIM), lambda b, t, h: (b * nt + t, h)),
        out_shape=jax.ShapeDtypeStruct((n, H * HEAD_DIM), BF16),
        scratch_shapes=[pltpu.VMEM((T, HEAD_DIM), F32),
                        pltpu.VMEM((2 * T, HEAD_DIM), F32),
                        pltpu.VMEM((2 * T, HEAD_DIM), F32),
                        pltpu.VMEM((3, T, HEAD_DIM), F32),
                        pltpu.VMEM((3, T, LANES), F32),
                        pltpu.VMEM((3, T, LANES), F32)],
        compiler_params=_params(3, 56),
        name="dilated_attn",
    )(proj, proj, proj, proj, proj, gq, gk, tab, tab)


def _rope_table(seq):
    half = ROPE_DIM // 2
    inv = ROPE_THETA ** (-jnp.arange(half, dtype=F32) * 2.0 / ROPE_DIM)
    ang = jnp.arange(seq, dtype=F32)[:, None] * inv[None, :]
    cos, sin = jnp.cos(ang), jnp.sin(ang)
    z = lambda w: jnp.zeros((seq, w), F32)
    c = jnp.concatenate([cos, cos, jnp.ones((seq, LANES - ROPE_DIM), F32)], axis=1)
    s1 = jnp.concatenate([-sin, z(LANES - half)], axis=1)
    s2 = jnp.concatenate([z(half), sin, z(LANES - ROPE_DIM)], axis=1)
    return jnp.concatenate([c, s1, s2], axis=1)


def _mlstm_body(mq_ref, mk_ref, mv_ref, mo_ref, gt_ref, gb_ref, cw_ref, cb_ref, og_ref, o_ref,
                prev_s, c_s, n_s, m_s):
    L = ML_CHUNK
    QW = ML_HEADS * ML_QK

    @pl.when(pl.program_id(1) == 0)
    def _():
        prev_s[...] = jnp.zeros_like(prev_s)
        c_s[...] = jnp.zeros_like(c_s)
        n_s[...] = jnp.zeros_like(n_s)
        m_s[...] = jnp.zeros_like(m_s)

    cur = jnp.concatenate([mq_ref[...], mk_ref[...]], axis=1)
    prv = prev_s[...]
    rid = lax.broadcasted_iota(I32, (L, 2 * QW), 0)
    conv = cur * cw_ref[ML_CONV - 1:ML_CONV, :] + cb_ref[...]
    for s in range(1, ML_CONV):
        shifted = jnp.where(rid < s, pltpu.roll(prv, s, 0), pltpu.roll(cur, s, 0))
        conv = conv + shifted * cw_ref[ML_CONV - 1 - s:ML_CONV - s, :]
    prev_s[...] = cur
    qk = conv * jax.nn.sigmoid(conv)

    gates = gt_ref[...] + gb_ref[...]
    logf = jnp.minimum(gates, 0.0) - jnp.log(1.0 + jnp.exp(-jnp.abs(gates)))
    ti = lax.broadcasted_iota(I32, (L, L), 0)
    si = lax.broadcasted_iota(I32, (L, L), 1)
    causal = si <= ti
    tril = jnp.where(causal, 1.0, 0.0).astype(F32)
    bmat = jnp.dot(tril, logf, precision=lax.Precision.HIGHEST, preferred_element_type=F32)
    rmat = gates - pltpu.roll(bmat, LANES - ML_HEADS, 1)
    rmat_t = rmat.T

    for h in range(ML_HEADS):
        q = qk[:, h * ML_QK:(h + 1) * ML_QK]
        k = qk[:, QW + h * ML_QK:QW + (h + 1) * ML_QK] * (1.0 / math.sqrt(ML_QK))
        v = mv_ref[:, h * ML_V:(h + 1) * ML_V]
        qb, kb, vb = q.astype(BF16), k.astype(BF16), v.astype(BF16)
        b_col = bmat[:, ML_HEADS + h:ML_HEADS + h + 1]
        r_col = rmat[:, h:h + 1]
        r_row = rmat_t[h:h + 1, :]
        g_tot = b_col[L - 1:L, :]
        m_prev = m_s[h]
        c_prev = c_s[h]
        n_prev = n_s[h]

        dmat = jnp.where(causal, b_col + r_row, NEG)
        inter = b_col + m_prev
        m_t = jnp.maximum(inter, jnp.max(dmat, axis=-1, keepdims=True))
        qkt = lax.dot_general(qb, kb, (((1,), (1,)), ((), ())), preferred_element_type=F32)
        pw = jnp.exp(dmat - m_t) * qkt
        wi = jnp.exp(inter - m_t)
        num = wi * jnp.dot(qb, c_prev.astype(BF16), preferred_element_type=F32) \
            + jnp.dot(pw.astype(BF16), vb, preferred_element_type=F32)
        den = wi * jnp.sum(q * n_prev, axis=-1, keepdims=True) + jnp.sum(pw, axis=-1, keepdims=True)
        hh = num / jnp.maximum(jnp.abs(den), jnp.exp(-m_t))
        ms = jnp.mean(hh * hh, axis=-1, keepdims=True)
        hn = hh * lax.rsqrt(ms + NORM_EPS) * og_ref[:, h * ML_V:(h + 1) * ML_V]
        hn = hn * jax.nn.sigmoid(mo_ref[:, h * ML_V:(h + 1) * ML_V])
        o_ref[:, h * ML_V:(h + 1) * ML_V] = hn.astype(o_ref.dtype)

        a_col = g_tot + r_col
        m_loc = jnp.max(a_col, axis=0, keepdims=True)
        wk = jnp.exp(a_col - m_loc)
        wkk = wk * k
        c_loc = lax.dot_general(wkk.astype(BF16), vb, (((0,), (0,)), ((), ())), preferred_element_type=F32)
        n_loc = jnp.sum(wkk, axis=0, keepdims=True)
        m_new = jnp.maximum(g_tot + m_prev, m_loc)
        d_old = jnp.exp(g_tot + m_prev - m_new)
        d_new = jnp.exp(m_loc - m_new)
        c_s[h] = d_old * c_prev + d_new * c_loc
        n_s[h] = d_old * n_prev + d_new * n_loc
        m_s[h] = m_new


def _mlstm(proj, gates, gate_bias, conv_w, conv_b, out_g, batch, seq):
    n = proj.shape[0]
    L = ML_CHUNK
    nc = seq // L
    QW = ML_HEADS * ML_QK
    VW = ML_HEADS * ML_V
    q_col = 3 * ATT_HEADS * HEAD_DIM
    assert q_col % QW == 0 and (q_col + 2 * QW) % VW == 0
    qi = q_col // QW
    vi = (q_col + 2 * QW) // VW
    row = lambda b, c: b * nc + c
    return pl.pallas_call(
        _mlstm_body,
        grid=(batch, nc),
        in_specs=[pl.BlockSpec((L, QW), lambda b, c: (row(b, c), qi)),
                  pl.BlockSpec((L, QW), lambda b, c: (row(b, c), qi + 1)),
                  pl.BlockSpec((L, VW), lambda b, c: (row(b, c), vi)),
                  pl.BlockSpec((L, VW), lambda b, c: (row(b, c), vi + 1)),
                  pl.BlockSpec((L, LANES), lambda b, c: (row(b, c), 0)),
                  pl.BlockSpec((1, LANES), lambda b, c: (0, 0)),
                  pl.BlockSpec((ML_CONV, 2 * QW), lambda b, c: (0, 0)),
                  pl.BlockSpec((1, 2 * QW), lambda b, c: (0, 0)),
                  pl.BlockSpec((1, VW), lambda b, c: (0, 0))],
        out_specs=pl.BlockSpec((L, VW), lambda b, c: (row(b, c), 0)),
        out_shape=jax.ShapeDtypeStruct((n, VW), BF16),
        scratch_shapes=[pltpu.VMEM((L, 2 * QW), F32),
                        pltpu.VMEM((ML_HEADS, ML_QK, ML_V), F32),
                        pltpu.VMEM((ML_HEADS, 1, ML_QK), F32),
                        pltpu.VMEM((ML_HEADS, 1, 1), F32)],
        compiler_params=_params(2, 32),
        name="mlstm",
    )(proj, proj, proj, proj, gates, gate_bias, conv_w, conv_b, out_g)


def _outproj_body(att_ref, hm_ref, x_ref, wa_ref, wm_ref, g_ref, rw_ref, rb_ref,
                  x1_ref, h2_ref, idx_ref, gate_ref, rank_ref, cnt_ref, carry):
    tm = x_ref.shape[0]

    @pl.when(pl.program_id(0) == 0)
    def _():
        carry[...] = jnp.zeros_like(carry)

    y = jnp.dot(att_ref[...], wa_ref[...], preferred_element_type=F32)
    y = y + jnp.dot(hm_ref[...], wm_ref[...], preferred_element_type=F32)
    x1 = x_ref[...] + y
    x1_ref[...] = x1
    ms = jnp.mean(x1 * x1, axis=-1, keepdims=True)
    h2 = x1 * lax.rsqrt(ms + NORM_EPS) * g_ref[...]
    half = h2.shape[1] // 2
    hb = h2.astype(BF16).astype(F32)
    lo = lax.shift_right_logical(lax.bitcast_convert_type(hb[:, :half], jnp.uint32), jnp.uint32(16))
    hi = lax.bitcast_convert_type(hb[:, half:], jnp.uint32) & jnp.uint32(0xFFFF0000)
    h2_ref[...] = lo | hi
    logits = jnp.dot(h2, rw_ref[...], precision=lax.Precision.HIGHEST,
                     preferred_element_type=F32) + rb_ref[...]
    lane = lax.broadcasted_iota(I32, (tm, LANES), 1)
    work = jnp.where(lane < N_EXPERTS, logits, NEG)
    vals, idxs = [], []
    onehot = jnp.zeros((tm, LANES), F32)
    for _ in range(TOP_K):
        mx = jnp.max(work, axis=-1, keepdims=True)
        ix = jnp.min(jnp.where(work == mx, lane, LANES), axis=-1, keepdims=True)
        sel = lane == ix
        onehot = jnp.where(sel, 1.0, onehot)
        work = jnp.where(sel, 2.0 * NEG, work)
        vals.append(mx)
        idxs.append(ix)
    es = [jnp.exp(v - vals[0]) for v in vals]
    tot = es[0] + es[1] + es[2] + es[3]

    ri = lax.broadcasted_iota(I32, (tm, tm), 0)
    ci = lax.broadcasted_iota(I32, (tm, tm), 1)
    strict = jnp.where(ci < ri, 1.0, 0.0).astype(BF16)
    cum = jnp.dot(strict, onehot.astype(BF16), preferred_element_type=F32) + carry[0:1, :]
    idx_out = jnp.zeros((tm, LANES), I32)
    gate_out = jnp.zeros((tm, LANES), F32)
    rank_out = jnp.zeros((tm, LANES), I32)
    for k in range(TOP_K):
        rk = jnp.sum(jnp.where(lane == idxs[k], cum, 0.0), axis=-1, keepdims=True)
        idx_out = jnp.where(lane == k, idxs[k], idx_out)
        gate_out = jnp.where(lane == k, es[k] / tot, gate_out)
        rank_out = jnp.where(lane == k, rk.astype(I32), rank_out)
    idx_ref[...] = idx_out
    gate_ref[...] = gate_out
    rank_ref[...] = rank_out
    new = carry[...] + jnp.sum(onehot, axis=0, keepdims=True)
    carry[...] = new
    cnt_ref[...] = new


def _outproj(att, hm, x2d, wa, wm, g, rw, rb, tm):
    n, d = x2d.shape
    row = lambda i: (i, 0)
    const = lambda i: (0, 0)
    return pl.pallas_call(
        _outproj_body,
        grid=(n // tm,),
        in_specs=[pl.BlockSpec((tm, att.shape[1]), row),
                  pl.BlockSpec((tm, hm.shape[1]), row),
                  pl.BlockSpec((tm, d), row),
                  pl.BlockSpec(wa.shape, const),
                  pl.BlockSpec(wm.shape, const),
                  pl.BlockSpec((1, d), const),
                  pl.BlockSpec((d, LANES), const),
                  pl.BlockSpec((1, LANES), const)],
        out_specs=[pl.BlockSpec((tm, d), row),
                   pl.BlockSpec((tm, d // 2), row),
                   pl.BlockSpec((tm, LANES), row),
                   pl.BlockSpec((tm, LANES), row),
                   pl.BlockSpec((tm, LANES), row),
                   pl.BlockSpec((8, LANES), const)],
        out_shape=[jax.ShapeDtypeStruct((n, d), F32),
                   jax.ShapeDtypeStruct((n, d // 2), jnp.uint32),
                   jax.ShapeDtypeStruct((n, LANES), I32),
                   jax.ShapeDtypeStruct((n, LANES), F32),
                   jax.ShapeDtypeStruct((n, LANES), I32),
                   jax.ShapeDtypeStruct((8, LANES), F32)],
        scratch_shapes=[pltpu.VMEM((8, LANES), F32)],
        compiler_params=_params(1, 56),
        name="outproj_router",
    )(att, hm, x2d, wa, wm, g, rw, rb)


def _split_glu_body(w_ref, g_ref, l_ref):
    grp = 2 * LANES
    r = lax.broadcasted_iota(I32, (grp, grp), 0)
    c = lax.broadcasted_iota(I32, (grp, grp), 1)
    src = jnp.where(c < LANES, 2 * c, 2 * (c - LANES) + 1)
    perm = jnp.where(r == src, 1.0, 0.0).astype(BF16)
    for j in range(w_ref.shape[2] // grp):
        blk = w_ref[0, :, j * grp:(j + 1) * grp].astype(BF16)
        out = jnp.dot(blk, perm, preferred_element_type=F32).astype(BF16)
        g_ref[0, :, j * LANES:(j + 1) * LANES] = out[:, :LANES]
        l_ref[0, :, j * LANES:(j + 1) * LANES] = out[:, LANES:]


def _split_glu(w_up, tr, tc):
    e, d, f2 = w_up.shape
    out = jax.ShapeDtypeStruct((e, d, f2 // 2), BF16)
    return pl.pallas_call(
        _split_glu_body,
        grid=(e, d // tr, f2 // tc),
        in_specs=[pl.BlockSpec((1, tr, tc), lambda a, i, j: (a, i, j))],
        out_specs=[pl.BlockSpec((1, tr, tc // 2), lambda a, i, j: (a, i, j)),
                   pl.BlockSpec((1, tr, tc // 2), lambda a, i, j: (a, i, j))],
        out_shape=[out, out],
        compiler_params=_params(3, 48),
        name="split_glu_weights",
    )(w_up)


MOE_SUB = 512


def _moe_body(te_ref, tv_ref, to_ref, tn_ref, rowtok_hbm, h2p_hbm, wg_ref, wl_ref, bg_ref, bl_ref, wd_ref, bd_ref,
              ys_ref, idx_smem, xp_buf, xb_buf, sem_idx, sem_rows):
    i = pl.program_id(0)
    f = pl.program_id(1)
    nt = pl.num_programs(0)
    tm, half = xp_buf.shape[1], xp_buf.shape[2]
    slot = lax.rem(i, 2)
    nslot = 1 - slot
    valid = tv_ref[i] > 0
    next_valid = jnp.logical_and(i + 1 < nt, tv_ref[jnp.minimum(i + 1, nt - 1)] > 0)

    def idx_copy(tile, s):
        return pltpu.make_async_copy(rowtok_hbm.at[tile], idx_smem.at[s], sem_idx.at[s])

    def issue_rows(s):
        def issue(r, carry):
            tok = idx_smem[s, r]
            pltpu.make_async_copy(h2p_hbm.at[pl.ds(tok, 1), :], xp_buf.at[s, pl.ds(r, 1), :],
                                  sem_rows.at[s]).start()
            return carry
        lax.fori_loop(0, tm, issue, 0, unroll=8)

    @pl.when(jnp.logical_and(jnp.logical_and(i == 0, f == 0), valid))
    def _():
        cp = idx_copy(0, 0)
        cp.start()
        cp.wait()
        issue_rows(0)

    @pl.when(jnp.logical_and(valid, f == 0))
    def _():
        pltpu.make_async_copy(h2p_hbm.at[pl.ds(0, tm), :], xp_buf.at[slot], sem_rows.at[slot]).wait()
        u = xp_buf[slot]
        xb_buf[:, :half] = lax.bitcast_convert_type(lax.shift_left(u, jnp.uint32(16)), F32).astype(BF16)
        xb_buf[:, half:] = lax.bitcast_convert_type(u & jnp.uint32(0xFFFF0000), F32).astype(BF16)

        @pl.when(next_valid)
        def _():
            idx_copy(i + 1, nslot).start()

    @pl.when(jnp.logical_and(jnp.logical_and(valid, f == 1), next_valid))
    def _():
        idx_copy(i + 1, nslot).wait()
        issue_rows(nslot)

    @pl.when(valid)
    def _():
        nrows = tn_ref[i]
        for sb in range(tm // MOE_SUB):
            rows = slice(sb * MOE_SUB, (sb + 1) * MOE_SUB)

            @pl.when(sb * MOE_SUB < nrows)
            def _(rows=rows):
                x = xb_buf[rows, :]
                hg = jnp.dot(x, wg_ref[0], preferred_element_type=F32) + bg_ref[0]
                hl = jnp.dot(x, wl_ref[0], preferred_element_type=F32) + bl_ref[0]
                glu = jnp.minimum(hg, SWIGLU_LIMIT)
                lin = jnp.clip(hl, -SWIGLU_LIMIT, SWIGLU_LIMIT)
                act = glu * jax.nn.sigmoid(SWIGLU_ALPHA * glu) * (lin + 1.0)
                y = jnp.dot(act.astype(BF16), wd_ref[0], preferred_element_type=F32)

                @pl.when(f == 0)
                def _():
                    ys_ref[rows, :] = y + bd_ref[0]

                @pl.when(f > 0)
                def _():
                    ys_ref[rows, :] += y


def _moe(tile_e, tile_valid, tile_out, tile_rows, row_tok, h2p, wg, wl, bg, bl, wd, bd, tm, fc):
    nt = row_tok.shape[0]
    half = h2p.shape[1]
    d = 2 * half
    ff = wg.shape[2]
    nf = ff // fc
    assert nf >= 2 and tm % MOE_SUB == 0

    def fidx(i, f, tv):
        return jnp.where(tv[i] > 0, f, nf - 1)

    grid_spec = pltpu.PrefetchScalarGridSpec(
        num_scalar_prefetch=4,
        grid=(nt, nf),
        in_specs=[pl.BlockSpec(memory_space=pl.ANY),
                  pl.BlockSpec(memory_space=pl.ANY),
                  pl.BlockSpec((1, d, fc), lambda i, f, te, tv, to, tn: (te[i], 0, fidx(i, f, tv))),
                  pl.BlockSpec((1, d, fc), lambda i, f, te, tv, to, tn: (te[i], 0, fidx(i, f, tv))),
                  pl.BlockSpec((1, 1, fc), lambda i, f, te, tv, to, tn: (te[i], 0, fidx(i, f, tv))),
                  pl.BlockSpec((1, 1, fc), lambda i, f, te, tv, to, tn: (te[i], 0, fidx(i, f, tv))),
                  pl.BlockSpec((1, fc, d), lambda i, f, te, tv, to, tn: (te[i], fidx(i, f, tv), 0)),
                  pl.BlockSpec((1, 1, d), lambda i, f, te, tv, to, tn: (te[i], 0, 0))],
        out_specs=pl.BlockSpec((tm, d), lambda i, f, te, tv, to, tn: (to[i], 0)),
        scratch_shapes=[pltpu.SMEM((2, tm), I32),
                        pltpu.VMEM((2, tm, half), jnp.uint32),
                        pltpu.VMEM((tm, d), BF16),
                        pltpu.SemaphoreType.DMA((2,)),
                        pltpu.SemaphoreType.DMA((2,))],
    )
    return pl.pallas_call(
        _moe_body,
        grid_spec=grid_spec,
        out_shape=jax.ShapeDtypeStruct((nt * tm, d), F32),
        compiler_params=_params(2, 56),
        name="moe_ffn",
    )(tile_e, tile_valid, tile_out, tile_rows, row_tok, h2p, wg, wl, bg, bl, wd, bd)


def _combine_body(dest_hbm, ys_hbm, x1_ref, gate_ref, o_ref, idx_smem, buf, sem_idx, sem_rows):
    i = pl.program_id(0)
    tm = x1_ref.shape[0]
    cp = pltpu.make_async_copy(dest_hbm.at[i], idx_smem, sem_idx)
    cp.start()
    cp.wait()
    for k in range(TOP_K):
        def issue(r, carry, k=k):
            row = idx_smem[k * tm + r]
            pltpu.make_async_copy(ys_hbm.at[pl.ds(row, 1), :], buf.at[k, pl.ds(r, 1), :], sem_rows).start()
            return carry
        lax.fori_loop(0, tm, issue, 0, unroll=8)
    for k in range(TOP_K):
        pltpu.make_async_copy(ys_hbm.at[pl.ds(0, tm), :], buf.at[k], sem_rows).wait()
    out = x1_ref[...]
    g = gate_ref[...]
    for k in range(TOP_K):
        out = out + g[:, k:k + 1] * buf[k]
    o_ref[...] = out


def _combine(dest_tiles, ys, x1, gates, tm):
    n, d = x1.shape
    return pl.pallas_call(
        _combine_body,
        grid=(n // tm,),
        in_specs=[pl.BlockSpec(memory_space=pl.ANY),
                  pl.BlockSpec(memory_space=pl.ANY),
                  pl.BlockSpec((tm, d), lambda i: (i, 0)),
                  pl.BlockSpec((tm, LANES), lambda i: (i, 0))],
        out_specs=pl.BlockSpec((tm, d), lambda i: (i, 0)),
        out_shape=jax.ShapeDtypeStruct((n, d), F32),
        scratch_shapes=[pltpu.SMEM((TOP_K * tm,), I32),
                        pltpu.VMEM((TOP_K, tm, d), F32),
                        pltpu.SemaphoreType.DMA,
                        pltpu.SemaphoreType.DMA],
        compiler_params=_params(1, 40),
        name="moe_combine",
    )(dest_tiles, ys, x1, gates)


def _pad_lanes(a, width=LANES):
    return jnp.pad(a, [(0, 0)] * (a.ndim - 1) + [(0, width - a.shape[-1])])


def kernel(x, attn_norm_g, w_in, q_norm_g, k_norm_g, ml_conv_w, ml_conv_b, ml_i_b, ml_f_b, ml_out_norm_g,
           w_out, ffn_norm_g, router_w, router_b, w_up, b_up, w_down, b_down):
    B, S, D = x.shape
    N = B * S
    assert S % ATT_TILE == 0 and attn_norm_g.shape[0] == 1
    att_w = ATT_HEADS * HEAD_DIM
    main_w = 3 * att_w + 2 * ML_HEADS * ML_QK + 2 * ML_HEADS * ML_V
    x2d = x.reshape(N, D)

    w_in0 = w_in[0]
    w_main = w_in0[:, :main_w].astype(BF16)
    w_gate = _pad_lanes(w_in0[:, main_w:]).astype(BF16)
    proj, gates = _inproj(x2d, attn_norm_g[0][None, :], w_main, w_gate, tm=1024, tn=1024)

    att = _attention(proj, q_norm_g[0][None, :], k_norm_g[0][None, :], _rope_table(S), B, S)

    gate_bias = _pad_lanes(jnp.concatenate([ml_i_b[0], ml_f_b[0]])[None, :]).astype(F32)
    hm = _mlstm(proj, gates, gate_bias, ml_conv_w[0], ml_conv_b[0][None, :], ml_out_norm_g[0][None, :], B, S)

    w_out0 = w_out[0].astype(BF16)
    x1, h2, top_idx, top_gate, top_rank, counts = _outproj(
        att, hm, x2d, w_out0[:att_w], w_out0[att_w:], ffn_norm_g[0][None, :],
        _pad_lanes(router_w[0]).astype(F32), _pad_lanes(router_b[0][None, :]).astype(F32), tm=512)

    tm_moe = 1024
    n_tiles = N * TOP_K // tm_moe + N_EXPERTS
    cnt = counts[0, :N_EXPERTS].astype(I32)
    ptiles = (cnt + tm_moe - 1) // tm_moe
    tend = jnp.cumsum(ptiles)
    tstart = tend - ptiles
    total = tend[-1]
    pstart = tstart * tm_moe
    e_idx = top_idx[:, :TOP_K]
    dest = pstart[e_idx] + top_rank[:, :TOP_K]
    tok = jnp.broadcast_to(jnp.arange(N, dtype=I32)[:, None], (N, TOP_K))
    row_tok = jnp.zeros((n_tiles * tm_moe,), I32).at[dest.reshape(-1)].set(tok.reshape(-1))
    tile_ids = jnp.arange(n_tiles, dtype=I32)
    tile_e = jnp.minimum(jnp.sum((tile_ids[:, None] >= tend[None, :]).astype(I32), axis=1), N_EXPERTS - 1)
    tile_valid = (tile_ids < total).astype(I32)
    tile_out = jnp.clip(tile_ids, 0, jnp.maximum(total - 1, 0)).astype(I32)
    tile_rows = jnp.clip(cnt[tile_e] - (tile_ids - tstart[tile_e]) * tm_moe, 0, tm_moe).astype(I32)

    wg, wl = _split_glu(w_up[0], tr=1024, tc=2048)
    bg = b_up[0][:, None, 0::2].astype(F32)
    bl = b_up[0][:, None, 1::2].astype(F32)
    wd = w_down[0].astype(BF16)
    bd = b_down[0][:, None, :].astype(F32)
    ys = _moe(tile_e, tile_valid, tile_out, tile_rows, row_tok.reshape(n_tiles, tm_moe), h2, wg, wl, bg, bl, wd, bd,
              tm=tm_moe, fc=512)

    tm_c = 256
    dest_tiles = dest.reshape(N // tm_c, tm_c, TOP_K).transpose(0, 2, 1).reshape(N // tm_c, TOP_K * tm_c)
    out = _combine(dest_tiles, ys, x1, top_gate, tm=tm_c)
    return out.reshape(B, S, D)
```

```python
import functools
import math

import jax
import jax.numpy as jnp
from jax import lax
from jax.experimental import pallas as pl
from jax.experimental.pallas import tpu as pltpu

F32 = jnp.float32
BF16 = jnp.bfloat16
I32 = jnp.int32

NORM_EPS = 1e-6
ATT_HEADS = 8
HEAD_DIM = 128
ROPE_DIM = 32
ROPE_THETA = 500000.0
DILATIONS = (1, 4, 16)
BAND = 128
ATT_TILE = 2048
ML_HEADS = 4
ML_QK = 128
ML_V = 256
ML_CHUNK = 128
ML_CONV = 4
N_EXPERTS = 32
TOP_K = 4
SWIGLU_ALPHA = 1.702
SWIGLU_LIMIT = 7.0
NEG = -1e30
LANES = 128
MIB = 2 ** 20


def _params(n_axes, vmem_mib):
    return pltpu.CompilerParams(dimension_semantics=("arbitrary",) * n_axes,
                                vmem_limit_bytes=vmem_mib * MIB)


def _inproj_body(x_ref, g_ref, w_ref, wg_ref, o_ref, gate_ref, h_ref):
    @pl.when(pl.program_id(1) == 0)
    def _():
        xf = x_ref[...]
        ms = jnp.mean(xf * xf, axis=-1, keepdims=True)
        h = (xf * lax.rsqrt(ms + NORM_EPS) * g_ref[...]).astype(BF16)
        h_ref[...] = h
        gate_ref[...] = jnp.dot(h, wg_ref[...], preferred_element_type=F32)

    o_ref[...] = jnp.dot(h_ref[...], w_ref[...], preferred_element_type=F32)


def _inproj(x2d, g, w_main, w_gate, tm, tn):
    n, d = x2d.shape
    wcols = w_main.shape[1]
    return pl.pallas_call(
        _inproj_body,
        grid=(n // tm, wcols // tn),
        in_specs=[pl.BlockSpec((tm, d), lambda i, j: (i, 0)),
                  pl.BlockSpec((1, d), lambda i, j: (0, 0)),
                  pl.BlockSpec((d, tn), lambda i, j: (0, j)),
                  pl.BlockSpec((d, LANES), lambda i, j: (0, 0))],
        out_specs=[pl.BlockSpec((tm, tn), lambda i, j: (i, j)),
                   pl.BlockSpec((tm, LANES), lambda i, j: (i, 0))],
        out_shape=[jax.ShapeDtypeStruct((n, wcols), F32),
                   jax.ShapeDtypeStruct((n, LANES), F32)],
        scratch_shapes=[pltpu.VMEM((tm, d), BF16)],
        compiler_params=_params(2, 48),
        name="inproj",
    )(x2d, g, w_main, w_gate)


def _attn_body(q_ref, kc_ref, kp_ref, vc_ref, vp_ref, gq_ref, gk_ref, tc_ref, tp_ref, o_ref,
               qb, kb, vb, acc_s, m_s, l_s):
    t = pl.program_id(1)
    T = ATT_TILE
    CH = 256
    scale = 1.0 / math.sqrt(HEAD_DIM)

    ones = jnp.ones((HEAD_DIM, LANES), BF16)

    def norm_rope(x, g, tab):
        sq = x * x
        hi = sq.astype(BF16)
        lo = (sq - hi.astype(F32)).astype(BF16)
        ssq = jnp.dot(hi, ones, preferred_element_type=F32) + jnp.dot(lo, ones, preferred_element_type=F32)
        y = x * lax.rsqrt(ssq * (1.0 / HEAD_DIM) + NORM_EPS) * g
        c = tab[:, 0:LANES]
        s1 = tab[:, LANES:2 * LANES]
        s2 = tab[:, 2 * LANES:3 * LANES]
        return y * c + pltpu.roll(y, LANES - ROPE_DIM // 2, 1) * s1 + pltpu.roll(y, ROPE_DIM // 2, 1) * s2

    def prep(c, carry):
        rows = pl.ds(pl.multiple_of(c * CH, CH), CH)
        prow = pl.ds(pl.multiple_of(c * CH + T, CH), CH)
        tabc = tc_ref[rows, :]
        qb[rows, :] = norm_rope(q_ref[rows, :], gq_ref[...], tabc) * scale
        kb[prow, :] = norm_rope(kc_ref[rows, :], gk_ref[...], tabc)
        kb[rows, :] = norm_rope(kp_ref[rows, :], gk_ref[...], tp_ref[rows, :])
        vb[rows, :] = vp_ref[rows, :]
        vb[prow, :] = vc_ref[rows, :]
        return carry

    lax.fori_loop(0, T // CH, prep, 0)

    def unit(bidx, d, u):
        seg = BAND * d
        if d == 1:
            sg, qoff = u, pl.multiple_of(u * BAND, BAND)
        elif d == DILATIONS[-1]:
            sg, qoff = 0, u
        else:
            sg = lax.shift_right_logical(u, 2)
            qoff = sg * seg + lax.bitwise_and(u, d - 1)

        def rows(off):
            return pl.ds(off, BAND) if d == 1 else pl.ds(off, BAND, stride=d)

        qs = qb[rows(qoff), :].astype(BF16)
        kk = jnp.concatenate([kb[rows(T + qoff - seg), :], kb[rows(T + qoff), :]], axis=0).astype(BF16)
        vv = jnp.concatenate([vb[rows(T + qoff - seg), :], vb[rows(T + qoff), :]], axis=0).astype(BF16)
        s = lax.dot_general(qs, kk, (((1,), (1,)), ((), ())), preferred_element_type=F32)
        ii = lax.broadcasted_iota(I32, (BAND, 2 * BAND), 0)
        jj = lax.broadcasted_iota(I32, (BAND, 2 * BAND), 1)
        dist = BAND + ii - jj
        first = jnp.logical_and(t == 0, sg == 0)
        jmin = jnp.where(first, BAND, 0)
        valid = (dist >= 0) & (dist <= BAND) & (jj >= jmin)
        s = jnp.where(valid, s, NEG)
        m = jnp.max(s, axis=-1, keepdims=True)
        p = jnp.exp(s - m)
        l = jnp.sum(p, axis=-1, keepdims=True)
        acc = jnp.dot(p.astype(BF16), vv, preferred_element_type=F32)
        acc_s.at[bidx][rows(qoff), :] = acc
        m_s.at[bidx][rows(qoff), :] = jnp.broadcast_to(m, (BAND, LANES))
        l_s.at[bidx][rows(qoff), :] = jnp.broadcast_to(l, (BAND, LANES))

    for bidx, d in enumerate(DILATIONS):
        def body(u, carry, bidx=bidx, d=d):
            unit(bidx, d, u)
            return carry
        lax.fori_loop(0, T // BAND, body, 0, unroll=4)

    def mix(c, carry):
        rows = pl.ds(pl.multiple_of(c * CH, CH), CH)
        ms = [m_s[b, rows, :] for b in range(3)]
        mx = jnp.maximum(jnp.maximum(ms[0], ms[1]), ms[2])
        num = jnp.zeros((CH, LANES), F32)
        den = jnp.zeros((CH, LANES), F32)
        for b in range(3):
            w = jnp.exp(ms[b] - mx)
            num = num + w * acc_s[b, rows, :]
            den = den + w * l_s[b, rows, :]
        o_ref[rows, :] = (num / den).astype(o_ref.dtype)
        return carry

    lax.fori_loop(0, T // CH, mix, 0)


def _attention(proj, gq, gk, tab, batch, seq):
    n = proj.shape[0]
    T = ATT_TILE
    nt = seq // T
    H = ATT_HEADS

    def cur(col0):
        return pl.BlockSpec((T, HEAD_DIM), lambda b, t, h: (b * nt + t, col0 + h))

    def prev(col0):
        return pl.BlockSpec((T, HEAD_DIM), lambda b, t, h: (b * nt + jnp.maximum(t - 1, 0), col0 + h))

    return pl.pallas_call(
        _attn_body,
        grid=(batch, nt, H),
        in_specs=[cur(0), cur(H), prev(H), cur(2 * H), prev(2 * H),
                  pl.BlockSpec((1, HEAD_DIM), lambda b, t, h: (0, 0)),
                  pl.BlockSpec((1, HEAD_DIM), lambda b, t, h: (0, 0)),
                  pl.BlockSpec((T, 3 * LANES), lambda b, t, h: (t, 0)),
                  pl.BlockSpec((T, 3 * LANES), lambda b, t, h: (jnp.maximum(t - 1, 0), 0))],
        out_specs=pl.BlockSpec((T, HEAD_DIM), lambda b, t, h: (b * nt + t, h)),
        out_shape=jax.ShapeDtypeStruct((n, H * HEAD_DIM), BF16),
        scratch_shapes=[pltpu.VMEM((T, HEAD_DIM), F32),
                        pltpu.VMEM((2 * T, HEAD_DIM), F32),
                        pltpu.VMEM((2 * T, HEAD_DIM), F32),
                        pltpu.VMEM((3, T, HEAD_DIM), F32),
                        pltpu.VMEM((3, T, LANES), F32),
                        pltpu.VMEM((3, T, LANES), F32)],
        compiler_params=_params(3, 56),
        name="dilated_attn",
    )(proj, proj, proj, proj, proj, gq, gk, tab, tab)


def _rope_table(seq):
    half = ROPE_DIM // 2
    inv = ROPE_THETA ** (-jnp.arange(half, dtype=F32) * 2.0 / ROPE_DIM)
    ang = jnp.arange(seq, dtype=F32)[:, None] * inv[None, :]
    cos, sin = jnp.cos(ang), jnp.sin(ang)
    z = lambda w: jnp.zeros((seq, w), F32)
    c = jnp.concatenate([cos, cos, jnp.ones((seq, LANES - ROPE_DIM), F32)], axis=1)
    s1 = jnp.concatenate([-sin, z(LANES - half)], axis=1)
    s2 = jnp.concatenate([z(half), sin, z(LANES - ROPE_DIM)], axis=1)
    return jnp.concatenate([c, s1, s2], axis=1)


def _mlstm_body(mq_ref, mk_ref, mv_ref, mo_ref, gt_ref, gb_ref, cw_ref, cb_ref, og_ref, o_ref,
                prev_s, c_s, n_s, m_s):
    L = ML_CHUNK
    QW = ML_HEADS * ML_QK

    @pl.when(pl.program_id(1) == 0)
    def _():
        prev_s[...] = jnp.zeros_like(prev_s)
        c_s[...] = jnp.zeros_like(c_s)
        n_s[...] = jnp.zeros_like(n_s)
        m_s[...] = jnp.zeros_like(m_s)

    cur = jnp.concatenate([mq_ref[...], mk_ref[...]], axis=1)
    prv = prev_s[...]
    rid = lax.broadcasted_iota(I32, (L, 2 * QW), 0)
    conv = cur * cw_ref[ML_CONV - 1:ML_CONV, :] + cb_ref[...]
    for s in range(1, ML_CONV):
        shifted = jnp.where(rid < s, pltpu.roll(prv, s, 0), pltpu.roll(cur, s, 0))
        conv = conv + shifted * cw_ref[ML_CONV - 1 - s:ML_CONV - s, :]
    prev_s[...] = cur
    qk = conv * jax.nn.sigmoid(conv)

    gates = gt_ref[...] + gb_ref[...]
    logf = jnp.minimum(gates, 0.0) - jnp.log(1.0 + jnp.exp(-jnp.abs(gates)))
    ti = lax.broadcasted_iota(I32, (L, L), 0)
    si = lax.broadcasted_iota(I32, (L, L), 1)
    causal = si <= ti
    tril = jnp.where(causal, 1.0, 0.0).astype(F32)
    bmat = jnp.dot(tril, logf, precision=lax.Precision.HIGHEST, preferred_element_type=F32)
    rmat = gates - pltpu.roll(bmat, LANES - ML_HEADS, 1)
    rmat_t = rmat.T

    for h in range(ML_HEADS):
        q = qk[:, h * ML_QK:(h + 1) * ML_QK]
        k = qk[:, QW + h * ML_QK:QW + (h + 1) * ML_QK] * (1.0 / math.sqrt(ML_QK))
        v = mv_ref[:, h * ML_V:(h + 1) * ML_V]
        qb, kb, vb = q.astype(BF16), k.astype(BF16), v.astype(BF16)
        b_col = bmat[:, ML_HEADS + h:ML_HEADS + h + 1]
        r_col = rmat[:, h:h + 1]
        r_row = rmat_t[h:h + 1, :]
        g_tot = b_col[L - 1:L, :]
        m_prev = m_s[h]
        c_prev = c_s[h]
        n_prev = n_s[h]

        dmat = jnp.where(causal, b_col + r_row, NEG)
        inter = b_col + m_prev
        m_t = jnp.maximum(inter, jnp.max(dmat, axis=-1, keepdims=True))
        qkt = lax.dot_general(qb, kb, (((1,), (1,)), ((), ())), preferred_element_type=F32)
        pw = jnp.exp(dmat - m_t) * qkt
        wi = jnp.exp(inter - m_t)
        num = wi * jnp.dot(qb, c_prev.astype(BF16), preferred_element_type=F32) \
            + jnp.dot(pw.astype(BF16), vb, preferred_element_type=F32)
        den = wi * jnp.sum(q * n_prev, axis=-1, keepdims=True) + jnp.sum(pw, axis=-1, keepdims=True)
        hh = num / jnp.maximum(jnp.abs(den), jnp.exp(-m_t))
        ms = jnp.mean(hh * hh, axis=-1, keepdims=True)
        hn = hh * lax.rsqrt(ms + NORM_EPS) * og_ref[:, h * ML_V:(h + 1) * ML_V]
        hn = hn * jax.nn.sigmoid(mo_ref[:, h * ML_V:(h + 1) * ML_V])
        o_ref[:, h * ML_V:(h + 1) * ML_V] = hn.astype(o_ref.dtype)

        a_col = g_tot + r_col
        m_loc = jnp.max(a_col, axis=0, keepdims=True)
        wk = jnp.exp(a_col - m_loc)
        wkk = wk * k
        c_loc = lax.dot_general(wkk.astype(BF16), vb, (((0,), (0,)), ((), ())), preferred_element_type=F32)
        n_loc = jnp.sum(wkk, axis=0, keepdims=True)
        m_new = jnp.maximum(g_tot + m_prev, m_loc)
        d_old = jnp.exp(g_tot + m_prev - m_new)
        d_new = jnp.exp(m_loc - m_new)
        c_s[h] = d_old * c_prev + d_new * c_loc
        n_s[h] = d_old * n_prev + d_new * n_loc
        m_s[h] = m_new


def _mlstm(proj, gates, gate_bias, conv_w, conv_b, out_g, batch, seq):
    n = proj.shape[0]
    L = ML_CHUNK
    nc = seq // L
    QW = ML_HEADS * ML_QK
    VW = ML_HEADS * ML_V
    q_col = 3 * ATT_HEADS * HEAD_DIM
    assert q_col % QW == 0 and (q_col + 2 * QW) % VW == 0
    qi = q_col // QW
    vi = (q_col + 2 * QW) // VW
    row = lambda b, c: b * nc + c
    return pl.pallas_call(
        _mlstm_body,
        grid=(batch, nc),
        in_specs=[pl.BlockSpec((L, QW), lambda b, c: (row(b, c), qi)),
                  pl.BlockSpec((L, QW), lambda b, c: (row(b, c), qi + 1)),
                  pl.BlockSpec((L, VW), lambda b, c: (row(b, c), vi)),
                  pl.BlockSpec((L, VW), lambda b, c: (row(b, c), vi + 1)),
                  pl.BlockSpec((L, LANES), lambda b, c: (row(b, c), 0)),
                  pl.BlockSpec((1, LANES), lambda b, c: (0, 0)),
                  pl.BlockSpec((ML_CONV, 2 * QW), lambda b, c: (0, 0)),
                  pl.BlockSpec((1, 2 * QW), lambda b, c: (0, 0)),
                  pl.BlockSpec((1, VW), lambda b, c: (0, 0))],
        out_specs=pl.BlockSpec((L, VW), lambda b, c: (row(b, c), 0)),
        out_shape=jax.ShapeDtypeStruct((n, VW), BF16),
        scratch_shapes=[pltpu.VMEM((L, 2 * QW), F32),
                        pltpu.VMEM((ML_HEADS, ML_QK, ML_V), F32),
                        pltpu.VMEM((ML_HEADS, 1, ML_QK), F32),
                        pltpu.VMEM((ML_HEADS, 1, 1), F32)],
        compiler_params=_params(2, 32),
        name="mlstm",
    )(proj, proj, proj, proj, gates, gate_bias, conv_w, conv_b, out_g)


def _outproj_body(att_ref, hm_ref, x_ref, wa_ref, wm_ref, g_ref, rw_ref, rb_ref,
                  x1_ref, h2_ref, idx_ref, gate_ref, rank_ref, cnt_ref, carry):
    tm = x_ref.shape[0]

    @pl.when(pl.program_id(0) == 0)
    def _():
        carry[...] = jnp.zeros_like(carry)

    y = jnp.dot(att_ref[...], wa_ref[...], preferred_element_type=F32)
    y = y + jnp.dot(hm_ref[...], wm_ref[...], preferred_element_type=F32)
    x1 = x_ref[...] + y
    x1_ref[...] = x1
    ms = jnp.mean(x1 * x1, axis=-1, keepdims=True)
    h2 = x1 * lax.rsqrt(ms + NORM_EPS) * g_ref[...]
    half = h2.shape[1] // 2
    hb = h2.astype(BF16).astype(F32)
    lo = lax.shift_right_logical(lax.bitcast_convert_type(hb[:, :half], jnp.uint32), jnp.uint32(16))
    hi = lax.bitcast_convert_type(hb[:, half:], jnp.uint32) & jnp.uint32(0xFFFF0000)
    packed = lo | hi
    for s in range(half // LANES):
        h2_ref[pl.ds(s, tm, stride=half // LANES), :] = packed[:, s * LANES:(s + 1) * LANES]
    logits = jnp.dot(h2, rw_ref[...], precision=lax.Precision.HIGHEST,
                     preferred_element_type=F32) + rb_ref[...]
    lane = lax.broadcasted_iota(I32, (tm, LANES), 1)
    work = jnp.where(lane < N_EXPERTS, logits, NEG)
    vals, idxs = [], []
    onehot = jnp.zeros((tm, LANES), F32)
    for _ in range(TOP_K):
        mx = jnp.max(work, axis=-1, keepdims=True)
        ix = jnp.min(jnp.where(work == mx, lane, LANES), axis=-1, keepdims=True)
        sel = lane == ix
        onehot = jnp.where(sel, 1.0, onehot)
        work = jnp.where(sel, 2.0 * NEG, work)
        vals.append(mx)
        idxs.append(ix)
    es = [jnp.exp(v - vals[0]) for v in vals]
    tot = es[0] + es[1] + es[2] + es[3]

    ri = lax.broadcasted_iota(I32, (tm, tm), 0)
    ci = lax.broadcasted_iota(I32, (tm, tm), 1)
    strict = jnp.where(ci < ri, 1.0, 0.0).astype(BF16)
    cum = jnp.dot(strict, onehot.astype(BF16), preferred_element_type=F32) + carry[0:1, :]
    idx_out = jnp.zeros((tm, LANES), I32)
    gate_out = jnp.zeros((tm, LANES), F32)
    rank_out = jnp.zeros((tm, LANES), I32)
    for k in range(TOP_K):
        rk = jnp.sum(jnp.where(lane == idxs[k], cum, 0.0), axis=-1, keepdims=True)
        idx_out = jnp.where(lane == k, idxs[k], idx_out)
        gate_out = jnp.where(lane == k, es[k] / tot, gate_out)
        rank_out = jnp.where(lane == k, rk.astype(I32), rank_out)
    idx_ref[...] = idx_out
    gate_ref[...] = gate_out
    rank_ref[...] = rank_out
    new = carry[...] + jnp.sum(onehot, axis=0, keepdims=True)
    carry[...] = new
    cnt_ref[...] = new


def _outproj(att, hm, x2d, wa, wm, g, rw, rb, tm):
    n, d = x2d.shape
    row = lambda i: (i, 0)
    const = lambda i: (0, 0)
    return pl.pallas_call(
        _outproj_body,
        grid=(n // tm,),
        in_specs=[pl.BlockSpec((tm, att.shape[1]), row),
                  pl.BlockSpec((tm, hm.shape[1]), row),
                  pl.BlockSpec((tm, d), row),
                  pl.BlockSpec(wa.shape, const),
                  pl.BlockSpec(wm.shape, const),
                  pl.BlockSpec((1, d), const),
                  pl.BlockSpec((d, LANES), const),
                  pl.BlockSpec((1, LANES), const)],
        out_specs=[pl.BlockSpec((tm, d), row),
                   pl.BlockSpec((tm * (d // 2 // LANES), LANES), row),
                   pl.BlockSpec((tm, LANES), row),
                   pl.BlockSpec((tm, LANES), row),
                   pl.BlockSpec((tm, LANES), row),
                   pl.BlockSpec((8, LANES), const)],
        out_shape=[jax.ShapeDtypeStruct((n, d), F32),
                   jax.ShapeDtypeStruct((n * (d // 2 // LANES), LANES), jnp.uint32),
                   jax.ShapeDtypeStruct((n, LANES), I32),
                   jax.ShapeDtypeStruct((n, LANES), F32),
                   jax.ShapeDtypeStruct((n, LANES), I32),
                   jax.ShapeDtypeStruct((8, LANES), F32)],
        scratch_shapes=[pltpu.VMEM((8, LANES), F32)],
        compiler_params=_params(1, 56),
        name="outproj_router",
    )(att, hm, x2d, wa, wm, g, rw, rb)


def _split_glu_body(w_ref, g_ref, l_ref):
    grp = 2 * LANES
    r = lax.broadcasted_iota(I32, (grp, grp), 0)
    c = lax.broadcasted_iota(I32, (grp, grp), 1)
    src = jnp.where(c < LANES, 2 * c, 2 * (c - LANES) + 1)
    perm = jnp.where(r == src, 1.0, 0.0).astype(BF16)
    for j in range(w_ref.shape[2] // grp):
        blk = w_ref[0, :, j * grp:(j + 1) * grp].astype(BF16)
        out = jnp.dot(blk, perm, preferred_element_type=F32).astype(BF16)
        g_ref[0, :, j * LANES:(j + 1) * LANES] = out[:, :LANES]
        l_ref[0, :, j * LANES:(j + 1) * LANES] = out[:, LANES:]


def _split_glu(w_up, tr, tc):
    e, d, f2 = w_up.shape
    out = jax.ShapeDtypeStruct((e, d, f2 // 2), BF16)
    return pl.pallas_call(
        _split_glu_body,
        grid=(e, d // tr, f2 // tc),
        in_specs=[pl.BlockSpec((1, tr, tc), lambda a, i, j: (a, i, j))],
        out_specs=[pl.BlockSpec((1, tr, tc // 2), lambda a, i, j: (a, i, j)),
                   pl.BlockSpec((1, tr, tc // 2), lambda a, i, j: (a, i, j))],
        out_shape=[out, out],
        compiler_params=_params(3, 48),
        name="split_glu_weights",
    )(w_up)


MOE_SUB = 512


def _moe_body(te_ref, tv_ref, to_ref, tn_ref, rowtok_hbm, h2p_hbm, wg_ref, wl_ref, bg_ref, bl_ref, wd_ref, bd_ref,
              ys_ref, idx_smem, xp_buf, xb_buf, sem_idx, sem_rows):
    i = pl.program_id(0)
    f = pl.program_id(1)
    nt = pl.num_programs(0)
    tm, half = xb_buf.shape[0], xb_buf.shape[1] // 2
    spr = half // LANES
    slot = lax.rem(i, 2)
    nslot = 1 - slot
    valid = tv_ref[i] > 0
    next_valid = jnp.logical_and(i + 1 < nt, tv_ref[jnp.minimum(i + 1, nt - 1)] > 0)

    def idx_copy(tile, s):
        return pltpu.make_async_copy(rowtok_hbm.at[tile], idx_smem.at[s], sem_idx.at[s])

    def issue_rows(s):
        def issue(r, carry):
            tok = idx_smem[s, r]
            pltpu.make_async_copy(h2p_hbm.at[pl.ds(pl.multiple_of(tok * spr, spr), spr), :],
                                  xp_buf.at[s, pl.ds(pl.multiple_of(r * spr, spr), spr), :],
                                  sem_rows.at[s]).start()
            return carry
        lax.fori_loop(0, tm, issue, 0, unroll=8)

    @pl.when(jnp.logical_and(jnp.logical_and(i == 0, f == 0), valid))
    def _():
        cp = idx_copy(0, 0)
        cp.start()
        cp.wait()
        issue_rows(0)

    @pl.when(jnp.logical_and(valid, f == 0))
    def _():
        pltpu.make_async_copy(h2p_hbm.at[pl.ds(0, tm * spr), :], xp_buf.at[slot], sem_rows.at[slot]).wait()
        for c in range(spr):
            u = xp_buf.at[slot][pl.ds(c, tm, stride=spr), :]
            xb_buf[:, c * LANES:(c + 1) * LANES] = lax.bitcast_convert_type(
                lax.shift_left(u, jnp.uint32(16)), F32).astype(BF16)
            xb_buf[:, half + c * LANES:half + (c + 1) * LANES] = lax.bitcast_convert_type(
                u & jnp.uint32(0xFFFF0000), F32).astype(BF16)

        @pl.when(next_valid)
        def _():
            idx_copy(i + 1, nslot).start()

    @pl.when(jnp.logical_and(jnp.logical_and(valid, f == 1), next_valid))
    def _():
        idx_copy(i + 1, nslot).wait()
        issue_rows(nslot)

    @pl.when(valid)
    def _():
        nrows = tn_ref[i]
        for sb in range(tm // MOE_SUB):
            rows = slice(sb * MOE_SUB, (sb + 1) * MOE_SUB)

            @pl.when(sb * MOE_SUB < nrows)
            def _(rows=rows):
                x = xb_buf[rows, :]
                hg = jnp.dot(x, wg_ref[0], preferred_element_type=F32) + bg_ref[0]
                hl = jnp.dot(x, wl_ref[0], preferred_element_type=F32) + bl_ref[0]
                glu = jnp.minimum(hg, SWIGLU_LIMIT)
                lin = jnp.clip(hl, -SWIGLU_LIMIT, SWIGLU_LIMIT)
                act = glu * jax.nn.sigmoid(SWIGLU_ALPHA * glu) * (lin + 1.0)
                y = jnp.dot(act.astype(BF16), wd_ref[0], preferred_element_type=F32)

                @pl.when(f == 0)
                def _():
                    ys_ref[rows, :] = y + bd_ref[0]

                @pl.when(f > 0)
                def _():
                    ys_ref[rows, :] += y


def _moe(tile_e, tile_valid, tile_out, tile_rows, row_tok, h2p, wg, wl, bg, bl, wd, bd, tm, fc):
    nt = row_tok.shape[0]
    d = wg.shape[1]
    half = d // 2
    assert h2p.shape[1] == LANES and half % LANES == 0
    ff = wg.shape[2]
    nf = ff // fc
    assert nf >= 2 and tm % MOE_SUB == 0

    def fidx(i, f, tv):
        return jnp.where(tv[i] > 0, f, nf - 1)

    grid_spec = pltpu.PrefetchScalarGridSpec(
        num_scalar_prefetch=4,
        grid=(nt, nf),
        in_specs=[pl.BlockSpec(memory_space=pl.ANY),
                  pl.BlockSpec(memory_space=pl.ANY),
                  pl.BlockSpec((1, d, fc), lambda i, f, te, tv, to, tn: (te[i], 0, fidx(i, f, tv))),
                  pl.BlockSpec((1, d, fc), lambda i, f, te, tv, to, tn: (te[i], 0, fidx(i, f, tv))),
                  pl.BlockSpec((1, 1, fc), lambda i, f, te, tv, to, tn: (te[i], 0, fidx(i, f, tv))),
                  pl.BlockSpec((1, 1, fc), lambda i, f, te, tv, to, tn: (te[i], 0, fidx(i, f, tv))),
                  pl.BlockSpec((1, fc, d), lambda i, f, te, tv, to, tn: (te[i], fidx(i, f, tv), 0)),
                  pl.BlockSpec((1, 1, d), lambda i, f, te, tv, to, tn: (te[i], 0, 0))],
        out_specs=pl.BlockSpec((tm, d), lambda i, f, te, tv, to, tn: (to[i], 0)),
        scratch_shapes=[pltpu.SMEM((2, tm), I32),
                        pltpu.VMEM((2, tm * (half // LANES), LANES), jnp.uint32),
                        pltpu.VMEM((tm, d), BF16),
                        pltpu.SemaphoreType.DMA((2,)),
                        pltpu.SemaphoreType.DMA((2,))],
    )
    return pl.pallas_call(
        _moe_body,
        grid_spec=grid_spec,
        out_shape=jax.ShapeDtypeStruct((nt * tm, d), F32),
        compiler_params=_params(2, 56),
        name="moe_ffn",
    )(tile_e, tile_valid, tile_out, tile_rows, row_tok, h2p, wg, wl, bg, bl, wd, bd)


def _combine_body(dest_hbm, ys_hbm, x1_ref, gate_ref, o_ref, idx_smem, buf, sem_idx, sem_rows):
    i = pl.program_id(0)
    tm = x1_ref.shape[0]
    cp = pltpu.make_async_copy(dest_hbm.at[i], idx_smem, sem_idx)
    cp.start()
    cp.wait()
    for k in range(TOP_K):
        def issue(r, carry, k=k):
            row = idx_smem[k * tm + r]
            pltpu.make_async_copy(ys_hbm.at[pl.ds(row, 1), :], buf.at[k, pl.ds(r, 1), :], sem_rows).start()
            return carry
        lax.fori_loop(0, tm, issue, 0, unroll=8)
    for k in range(TOP_K):
        pltpu.make_async_copy(ys_hbm.at[pl.ds(0, tm), :], buf.at[k], sem_rows).wait()
    out = x1_ref[...]
    g = gate_ref[...]
    for k in range(TOP_K):
        out = out + g[:, k:k + 1] * buf[k]
    o_ref[...] = out


def _combine(dest_tiles, ys, x1, gates, tm):
    n, d = x1.shape
    return pl.pallas_call(
        _combine_body,
        grid=(n // tm,),
        in_specs=[pl.BlockSpec(memory_space=pl.ANY),
                  pl.BlockSpec(memory_space=pl.ANY),
                  pl.BlockSpec((tm, d), lambda i: (i, 0)),
                  pl.BlockSpec((tm, LANES), lambda i: (i, 0))],
        out_specs=pl.BlockSpec((tm, d), lambda i: (i, 0)),
        out_shape=jax.ShapeDtypeStruct((n, d), F32),
        scratch_shapes=[pltpu.SMEM((TOP_K * tm,), I32),
                        pltpu.VMEM((TOP_K, tm, d), F32),
                        pltpu.SemaphoreType.DMA,
                        pltpu.SemaphoreType.DMA],
        compiler_params=_params(1, 40),
        name="moe_combine",
    )(dest_tiles, ys, x1, gates)


def _pad_lanes(a, width=LANES):
    return jnp.pad(a, [(0, 0)] * (a.ndim - 1) + [(0, width - a.shape[-1])])


def kernel(x, attn_norm_g, w_in, q_norm_g, k_norm_g, ml_conv_w, ml_conv_b, ml_i_b, ml_f_b, ml_out_norm_g,
           w_out, ffn_norm_g, router_w, router_b, w_up, b_up, w_down, b_down):
    B, S, D = x.shape
    N = B * S
    assert S % ATT_TILE == 0 and attn_norm_g.shape[0] == 1
    att_w = ATT_HEADS * HEAD_DIM
    main_w = 3 * att_w + 2 * ML_HEADS * ML_QK + 2 * ML_HEADS * ML_V
    x2d = x.reshape(N, D)

    w_in0 = w_in[0]
    w_main = w_in0[:, :main_w].astype(BF16)
    w_gate = _pad_lanes(w_in0[:, main_w:]).astype(BF16)
    proj, gates = _inproj(x2d, attn_norm_g[0][None, :], w_main, w_gate, tm=1024, tn=1024)

    att = _attention(proj, q_norm_g[0][None, :], k_norm_g[0][None, :], _rope_table(S), B, S)

    gate_bias = _pad_lanes(jnp.concatenate([ml_i_b[0], ml_f_b[0]])[None, :]).astype(F32)
    hm = _mlstm(proj, gates, gate_bias, ml_conv_w[0], ml_conv_b[0][None, :], ml_out_norm_g[0][None, :], B, S)

    w_out0 = w_out[0].astype(BF16)
    x1, h2, top_idx, top_gate, top_rank, counts = _outproj(
        att, hm, x2d, w_out0[:att_w], w_out0[att_w:], ffn_norm_g[0][None, :],
        _pad_lanes(router_w[0]).astype(F32), _pad_lanes(router_b[0][None, :]).astype(F32), tm=512)

    tm_moe = 1024
    n_tiles = N * TOP_K // tm_moe + N_EXPERTS
    cnt = counts[0, :N_EXPERTS].astype(I32)
    ptiles = (cnt + tm_moe - 1) // tm_moe
    tend = jnp.cumsum(ptiles)
    tstart = tend - ptiles
    total = tend[-1]
    pstart = tstart * tm_moe
    e_idx = top_idx[:, :TOP_K]
    dest = pstart[e_idx] + top_rank[:, :TOP_K]
    tok = jnp.broadcast_to(jnp.arange(N, dtype=I32)[:, None], (N, TOP_K))
    row_tok = jnp.zeros((n_tiles * tm_moe,), I32).at[dest.reshape(-1)].set(tok.reshape(-1))
    tile_ids = jnp.arange(n_tiles, dtype=I32)
    tile_e = jnp.minimum(jnp.sum((tile_ids[:, None] >= tend[None, :]).astype(I32), axis=1), N_EXPERTS - 1)
    tile_valid = (tile_ids < total).astype(I32)
    tile_out = jnp.clip(tile_ids, 0, jnp.maximum(total - 1, 0)).astype(I32)
    tile_rows = jnp.clip(cnt[tile_e] - (tile_ids - tstart[tile_e]) * tm_moe, 0, tm_moe).astype(I32)

    wg, wl = _split_glu(w_up[0], tr=1024, tc=2048)
    bg = b_up[0][:, None, 0::2].astype(F32)
    bl = b_up[0][:, None, 1::2].astype(F32)
    wd = w_down[0].astype(BF16)
    bd = b_down[0][:, None, :].astype(F32)
    ys = _moe(tile_e, tile_valid, tile_out, tile_rows, row_tok.reshape(n_tiles, tm_moe), h2, wg, wl, bg, bl, wd, bd,
              tm=tm_moe, fc=512)

    tm_c = 256
    dest_tiles = dest.reshape(N // tm_c, tm_c, TOP_K).transpose(0, 2, 1).reshape(N // tm_c, TOP_K * tm_c)
    out = _combine(dest_tiles, ys, x1, top_gate, tm=tm_c)
    return out.reshape(B, S, D)
```

```python
import functools
import math

import jax
import jax.numpy as jnp
from jax import lax
from jax.experimental import pallas as pl
from jax.experimental.pallas import tpu as pltpu

F32 = jnp.float32
BF16 = jnp.bfloat16
I32 = jnp.int32

NORM_EPS = 1e-6
ATT_HEADS = 8
HEAD_DIM = 128
ROPE_DIM = 32
ROPE_THETA = 500000.0
DILATIONS = (1, 4, 16)
BAND = 128
ATT_TILE = 2048
ML_HEADS = 4
ML_QK = 128
ML_V = 256
ML_CHUNK = 128
ML_CONV = 4
N_EXPERTS = 32
TOP_K = 4
SWIGLU_ALPHA = 1.702
SWIGLU_LIMIT = 7.0
NEG = -1e30
LANES = 128
MIB = 2 ** 20


def _params(n_axes, vmem_mib):
    return pltpu.CompilerParams(dimension_semantics=("arbitrary",) * n_axes,
                                vmem_limit_bytes=vmem_mib * MIB)


def _inproj_body(x_ref, g_ref, w_ref, wg_ref, o_ref, gate_ref, h_ref):
    @pl.when(pl.program_id(1) == 0)
    def _():
        xf = x_ref[...]
        ms = jnp.mean(xf * xf, axis=-1, keepdims=True)
        h = (xf * lax.rsqrt(ms + NORM_EPS) * g_ref[...]).astype(BF16)
        h_ref[...] = h
        gate_ref[...] = jnp.dot(h, wg_ref[...], preferred_element_type=F32)

    o_ref[...] = jnp.dot(h_ref[...], w_ref[...], preferred_element_type=F32)


def _inproj(x2d, g, w_main, w_gate, tm, tn):
    n, d = x2d.shape
    wcols = w_main.shape[1]
    return pl.pallas_call(
        _inproj_body,
        grid=(n // tm, wcols // tn),
        in_specs=[pl.BlockSpec((tm, d), lambda i, j: (i, 0)),
                  pl.BlockSpec((1, d), lambda i, j: (0, 0)),
                  pl.BlockSpec((d, tn), lambda i, j: (0, j)),
                  pl.BlockSpec((d, LANES), lambda i, j: (0, 0))],
        out_specs=[pl.BlockSpec((tm, tn), lambda i, j: (i, j)),
                   pl.BlockSpec((tm, LANES), lambda i, j: (i, 0))],
        out_shape=[jax.ShapeDtypeStruct((n, wcols), F32),
                   jax.ShapeDtypeStruct((n, LANES), F32)],
        scratch_shapes=[pltpu.VMEM((tm, d), BF16)],
        compiler_params=_params(2, 48),
        name="inproj",
    )(x2d, g, w_main, w_gate)


def _attn_body(q_ref, kc_ref, kp_ref, vc_ref, vp_ref, gq_ref, gk_ref, tc_ref, tp_ref, o_ref,
               qb, kb, vb, acc_s, m_s, l_s):
    t = pl.program_id(1)
    T = ATT_TILE
    CH = 256
    scale = 1.0 / math.sqrt(HEAD_DIM)

    ones = jnp.ones((HEAD_DIM, LANES), BF16)

    def norm_rope(x, g, tab):
        sq = x * x
        hi = sq.astype(BF16)
        lo = (sq - hi.astype(F32)).astype(BF16)
        ssq = jnp.dot(hi, ones, preferred_element_type=F32) + jnp.dot(lo, ones, preferred_element_type=F32)
        y = x * lax.rsqrt(ssq * (1.0 / HEAD_DIM) + NORM_EPS) * g
        c = tab[:, 0:LANES]
        s1 = tab[:, LANES:2 * LANES]
        s2 = tab[:, 2 * LANES:3 * LANES]
        return y * c + pltpu.roll(y, LANES - ROPE_DIM // 2, 1) * s1 + pltpu.roll(y, ROPE_DIM // 2, 1) * s2

    def prep(c, carry):
        rows = pl.ds(pl.multiple_of(c * CH, CH), CH)
        prow = pl.ds(pl.multiple_of(c * CH + T, CH), CH)
        tabc = tc_ref[rows, :]
        qb[rows, :] = norm_rope(q_ref[rows, :], gq_ref[...], tabc) * scale
        kb[prow, :] = norm_rope(kc_ref[rows, :], gk_ref[...], tabc)
        kb[rows, :] = norm_rope(kp_ref[rows, :], gk_ref[...], tp_ref[rows, :])
        vb[rows, :] = vp_ref[rows, :]
        vb[prow, :] = vc_ref[rows, :]
        return carry

    lax.fori_loop(0, T // CH, prep, 0)

    def unit(bidx, d, u):
        seg = BAND * d
        if d == 1:
            sg, qoff = u, pl.multiple_of(u * BAND, BAND)
        elif d == DILATIONS[-1]:
            sg, qoff = 0, u
        else:
            sg = lax.shift_right_logical(u, 2)
            qoff = sg * seg + lax.bitwise_and(u, d - 1)

        def rows(off):
            return pl.ds(off, BAND) if d == 1 else pl.ds(off, BAND, stride=d)

        qs = qb[rows(qoff), :].astype(BF16)
        kk = jnp.concatenate([kb[rows(T + qoff - seg), :], kb[rows(T + qoff), :]], axis=0).astype(BF16)
        vv = jnp.concatenate([vb[rows(T + qoff - seg), :], vb[rows(T + qoff), :]], axis=0).astype(BF16)
        s = lax.dot_general(qs, kk, (((1,), (1,)), ((), ())), preferred_element_type=F32)
        ii = lax.broadcasted_iota(I32, (BAND, 2 * BAND), 0)
        jj = lax.broadcasted_iota(I32, (BAND, 2 * BAND), 1)
        dist = BAND + ii - jj
        first = jnp.logical_and(t == 0, sg == 0)
        jmin = jnp.where(first, BAND, 0)
        valid = (dist >= 0) & (dist <= BAND) & (jj >= jmin)
        s = jnp.where(valid, s, NEG)
        m = jnp.max(s, axis=-1, keepdims=True)
        p = jnp.exp(s - m)
        l = jnp.sum(p, axis=-1, keepdims=True)
        acc = jnp.dot(p.astype(BF16), vv, preferred_element_type=F32)
        acc_s.at[bidx][rows(qoff), :] = acc
        m_s.at[bidx][rows(qoff), :] = jnp.broadcast_to(m, (BAND, LANES))
        l_s.at[bidx][rows(qoff), :] = jnp.broadcast_to(l, (BAND, LANES))

    for bidx, d in enumerate(DILATIONS):
        def body(u, carry, bidx=bidx, d=d):
            unit(bidx, d, u)
            return carry
        lax.fori_loop(0, T // BAND, body, 0, unroll=8)

    def mix(c, carry):
        rows = pl.ds(pl.multiple_of(c * CH, CH), CH)
        ms = [m_s[b, rows, :] for b in range(3)]
        mx = jnp.maximum(jnp.maximum(ms[0], ms[1]), ms[2])
        num = jnp.zeros((CH, LANES), F32)
        den = jnp.zeros((CH, LANES), F32)
        for b in range(3):
            w = jnp.exp(ms[b] - mx)
            num = num + w * acc_s[b, rows, :]
            den = den + w * l_s[b, rows, :]
        o_ref[rows, :] = (num / den).astype(o_ref.dtype)
        return carry

    lax.fori_loop(0, T // CH, mix, 0)


def _attention(proj, gq, gk, tab, batch, seq):
    n = proj.shape[0]
    T = ATT_TILE
    nt = seq // T
    H = ATT_HEADS

    def cur(col0):
        return pl.BlockSpec((T, HEAD_DIM), lambda b, t, h: (b * nt + t, col0 + h))

    def prev(col0):
        return pl.BlockSpec((T, HEAD_DIM), lambda b, t, h: (b * nt + jnp.maximum(t - 1, 0), col0 + h))

    return pl.pallas_call(
        _attn_body,
        grid=(batch, nt, H),
        in_specs=[cur(0), cur(H), prev(H), cur(2 * H), prev(2 * H),
                  pl.BlockSpec((1, HEAD_DIM), lambda b, t, h: (0, 0)),
                  pl.BlockSpec((1, HEAD_DIM), lambda b, t, h: (0, 0)),
                  pl.BlockSpec((T, 3 * LANES), lambda b, t, h: (t, 0)),
                  pl.BlockSpec((T, 3 * LANES), lambda b, t, h: (jnp.maximum(t - 1, 0), 0))],
        out_specs=pl.BlockSpec((T, HEAD_DIM), lambda b, t, h: (b * nt + t, h)),
        out_shape=jax.ShapeDtypeStruct((n, H * HEAD_DIM), BF16),
        scratch_shapes=[pltpu.VMEM((T, HEAD_DIM), F32),
                        pltpu.VMEM((2 * T, HEAD_DIM), F32),
                        pltpu.VMEM((2 * T, HEAD_DIM), F32),
                        pltpu.VMEM((3, T, HEAD_DIM), F32),
                        pltpu.VMEM((3, T, LANES), F32),
                        pltpu.VMEM((3, T, LANES), F32)],
        compiler_params=_params(3, 56),
        name="dilated_attn",
    )(proj, proj, proj, proj, proj, gq, gk, tab, tab)


def _rope_table(seq):
    half = ROPE_DIM // 2
    inv = ROPE_THETA ** (-jnp.arange(half, dtype=F32) * 2.0 / ROPE_DIM)
    ang = jnp.arange(seq, dtype=F32)[:, None] * inv[None, :]
    cos, sin = jnp.cos(ang), jnp.sin(ang)
    z = lambda w: jnp.zeros((seq, w), F32)
    c = jnp.concatenate([cos, cos, jnp.ones((seq, LANES - ROPE_DIM), F32)], axis=1)
    s1 = jnp.concatenate([-sin, z(LANES - half)], axis=1)
    s2 = jnp.concatenate([z(half), sin, z(LANES - ROPE_DIM)], axis=1)
    return jnp.concatenate([c, s1, s2], axis=1)


def _mlstm_body(mq_ref, mk_ref, mv_ref, mo_ref, gt_ref, gb_ref, cw_ref, cb_ref, og_ref, o_ref,
                prev_s, c_s, n_s, m_s):
    L = ML_CHUNK
    QW = ML_HEADS * ML_QK

    @pl.when(pl.program_id(1) == 0)
    def _():
        prev_s[...] = jnp.zeros_like(prev_s)
        c_s[...] = jnp.zeros_like(c_s)
        n_s[...] = jnp.zeros_like(n_s)
        m_s[...] = jnp.zeros_like(m_s)

    cur = jnp.concatenate([mq_ref[...], mk_ref[...]], axis=1)
    prv = prev_s[...]
    rid = lax.broadcasted_iota(I32, (L, 2 * QW), 0)
    conv = cur * cw_ref[ML_CONV - 1:ML_CONV, :] + cb_ref[...]
    for s in range(1, ML_CONV):
        shifted = jnp.where(rid < s, pltpu.roll(prv, s, 0), pltpu.roll(cur, s, 0))
        conv = conv + shifted * cw_ref[ML_CONV - 1 - s:ML_CONV - s, :]
    prev_s[...] = cur
    qk = conv * jax.nn.sigmoid(conv)

    gates = gt_ref[...] + gb_ref[...]
    logf = jnp.minimum(gates, 0.0) - jnp.log(1.0 + jnp.exp(-jnp.abs(gates)))
    ti = lax.broadcasted_iota(I32, (L, L), 0)
    si = lax.broadcasted_iota(I32, (L, L), 1)
    causal = si <= ti
    tril = jnp.where(causal, 1.0, 0.0).astype(F32)
    bmat = jnp.dot(tril, logf, precision=lax.Precision.HIGHEST, preferred_element_type=F32)
    rmat = gates - pltpu.roll(bmat, LANES - ML_HEADS, 1)
    rmat_t = rmat.T

    for h in range(ML_HEADS):
        q = qk[:, h * ML_QK:(h + 1) * ML_QK]
        k = qk[:, QW + h * ML_QK:QW + (h + 1) * ML_QK] * (1.0 / math.sqrt(ML_QK))
        v = mv_ref[:, h * ML_V:(h + 1) * ML_V]
        qb, kb, vb = q.astype(BF16), k.astype(BF16), v.astype(BF16)
        b_col = bmat[:, ML_HEADS + h:ML_HEADS + h + 1]
        r_col = rmat[:, h:h + 1]
        r_row = rmat_t[h:h + 1, :]
        g_tot = b_col[L - 1:L, :]
        m_prev = m_s[h]
        c_prev = c_s[h]
        n_prev = n_s[h]

        dmat = jnp.where(causal, b_col + r_row, NEG)
        inter = b_col + m_prev
        m_t = jnp.maximum(inter, jnp.max(dmat, axis=-1, keepdims=True))
        qkt = lax.dot_general(qb, kb, (((1,), (1,)), ((), ())), preferred_element_type=F32)
        pw = jnp.exp(dmat - m_t) * qkt
        wi = jnp.exp(inter - m_t)
        num = wi * jnp.dot(qb, c_prev.astype(BF16), preferred_element_type=F32) \
            + jnp.dot(pw.astype(BF16), vb, preferred_element_type=F32)
        den = wi * jnp.sum(q * n_prev, axis=-1, keepdims=True) + jnp.sum(pw, axis=-1, keepdims=True)
        hh = num / jnp.maximum(jnp.abs(den), jnp.exp(-m_t))
        ms = jnp.mean(hh * hh, axis=-1, keepdims=True)
        hn = hh * lax.rsqrt(ms + NORM_EPS) * og_ref[:, h * ML_V:(h + 1) * ML_V]
        hn = hn * jax.nn.sigmoid(mo_ref[:, h * ML_V:(h + 1) * ML_V])
        o_ref[:, h * ML_V:(h + 1) * ML_V] = hn.astype(o_ref.dtype)

        a_col = g_tot + r_col
        m_loc = jnp.max(a_col, axis=0, keepdims=True)
        wk = jnp.exp(a_col - m_loc)
        wkk = wk * k
        c_loc = lax.dot_general(wkk.astype(BF16), vb, (((0,), (0,)), ((), ())), preferred_element_type=F32)
        n_loc = jnp.sum(wkk, axis=0, keepdims=True)
        m_new = jnp.maximum(g_tot + m_prev, m_loc)
        d_old = jnp.exp(g_tot + m_prev - m_new)
        d_new = jnp.exp(m_loc - m_new)
        c_s[h] = d_old * c_prev + d_new * c_loc
        n_s[h] = d_old * n_prev + d_new * n_loc
        m_s[h] = m_new


def _mlstm(proj, gates, gate_bias, conv_w, conv_b, out_g, batch, seq):
    n = proj.shape[0]
    L = ML_CHUNK
    nc = seq // L
    QW = ML_HEADS * ML_QK
    VW = ML_HEADS * ML_V
    q_col = 3 * ATT_HEADS * HEAD_DIM
    assert q_col % QW == 0 and (q_col + 2 * QW) % VW == 0
    qi = q_col // QW
    vi = (q_col + 2 * QW) // VW
    row = lambda b, c: b * nc + c
    return pl.pallas_call(
        _mlstm_body,
        grid=(batch, nc),
        in_specs=[pl.BlockSpec((L, QW), lambda b, c: (row(b, c), qi)),
                  pl.BlockSpec((L, QW), lambda b, c: (row(b, c), qi + 1)),
                  pl.BlockSpec((L, VW), lambda b, c: (row(b, c), vi)),
                  pl.BlockSpec((L, VW), lambda b, c: (row(b, c), vi + 1)),
                  pl.BlockSpec((L, LANES), lambda b, c: (row(b, c), 0)),
                  pl.BlockSpec((1, LANES), lambda b, c: (0, 0)),
                  pl.BlockSpec((ML_CONV, 2 * QW), lambda b, c: (0, 0)),
                  pl.BlockSpec((1, 2 * QW), lambda b, c: (0, 0)),
                  pl.BlockSpec((1, VW), lambda b, c: (0, 0))],
        out_specs=pl.BlockSpec((L, VW), lambda b, c: (row(b, c), 0)),
        out_shape=jax.ShapeDtypeStruct((n, VW), BF16),
        scratch_shapes=[pltpu.VMEM((L, 2 * QW), F32),
                        pltpu.VMEM((ML_HEADS, ML_QK, ML_V), F32),
                        pltpu.VMEM((ML_HEADS, 1, ML_QK), F32),
                        pltpu.VMEM((ML_HEADS, 1, 1), F32)],
        compiler_params=_params(2, 32),
        name="mlstm",
    )(proj, proj, proj, proj, gates, gate_bias, conv_w, conv_b, out_g)


def _outproj_body(att_ref, hm_ref, x_ref, wa_ref, wm_ref, g_ref, rw_ref, rb_ref,
                  x1_ref, h2_ref, idx_ref, gate_ref, rank_ref, cnt_ref, carry):
    tm = x_ref.shape[0]

    @pl.when(pl.program_id(0) == 0)
    def _():
        carry[...] = jnp.zeros_like(carry)

    y = jnp.dot(att_ref[...], wa_ref[...], preferred_element_type=F32)
    y = y + jnp.dot(hm_ref[...], wm_ref[...], preferred_element_type=F32)
    x1 = x_ref[...] + y
    x1_ref[...] = x1
    ms = jnp.mean(x1 * x1, axis=-1, keepdims=True)
    h2 = x1 * lax.rsqrt(ms + NORM_EPS) * g_ref[...]
    half = h2.shape[1] // 2
    hb = h2.astype(BF16).astype(F32)
    lo = lax.shift_right_logical(lax.bitcast_convert_type(hb[:, :half], jnp.uint32), jnp.uint32(16))
    hi = lax.bitcast_convert_type(hb[:, half:], jnp.uint32) & jnp.uint32(0xFFFF0000)
    packed = lo | hi
    for s in range(half // LANES):
        h2_ref[pl.ds(s, tm, stride=half // LANES), :] = packed[:, s * LANES:(s + 1) * LANES]
    logits = jnp.dot(h2, rw_ref[...], precision=lax.Precision.HIGHEST,
                     preferred_element_type=F32) + rb_ref[...]
    lane = lax.broadcasted_iota(I32, (tm, LANES), 1)
    work = jnp.where(lane < N_EXPERTS, logits, NEG)
    vals, idxs = [], []
    onehot = jnp.zeros((tm, LANES), F32)
    for _ in range(TOP_K):
        mx = jnp.max(work, axis=-1, keepdims=True)
        ix = jnp.min(jnp.where(work == mx, lane, LANES), axis=-1, keepdims=True)
        sel = lane == ix
        onehot = jnp.where(sel, 1.0, onehot)
        work = jnp.where(sel, 2.0 * NEG, work)
        vals.append(mx)
        idxs.append(ix)
    es = [jnp.exp(v - vals[0]) for v in vals]
    tot = es[0] + es[1] + es[2] + es[3]

    ri = lax.broadcasted_iota(I32, (tm, tm), 0)
    ci = lax.broadcasted_iota(I32, (tm, tm), 1)
    strict = jnp.where(ci < ri, 1.0, 0.0).astype(BF16)
    cum = jnp.dot(strict, onehot.astype(BF16), preferred_element_type=F32) + carry[0:1, :]
    idx_out = jnp.zeros((tm, LANES), I32)
    gate_out = jnp.zeros((tm, LANES), F32)
    rank_out = jnp.zeros((tm, LANES), I32)
    for k in range(TOP_K):
        rk = jnp.sum(jnp.where(lane == idxs[k], cum, 0.0), axis=-1, keepdims=True)
        idx_out = jnp.where(lane == k, idxs[k], idx_out)
        gate_out = jnp.where(lane == k, es[k] / tot, gate_out)
        rank_out = jnp.where(lane == k, rk.astype(I32), rank_out)
    idx_ref[...] = idx_out
    gate_ref[...] = gate_out
    rank_ref[...] = rank_out
    new = carry[...] + jnp.sum(onehot, axis=0, keepdims=True)
    carry[...] = new
    cnt_ref[...] = new


def _outproj(att, hm, x2d, wa, wm, g, rw, rb, tm):
    n, d = x2d.shape
    row = lambda i: (i, 0)
    const = lambda i: (0, 0)
    return pl.pallas_call(
        _outproj_body,
        grid=(n // tm,),
        in_specs=[pl.BlockSpec((tm, att.shape[1]), row),
                  pl.BlockSpec((tm, hm.shape[1]), row),
                  pl.BlockSpec((tm, d), row),
                  pl.BlockSpec(wa.shape, const),
                  pl.BlockSpec(wm.shape, const),
                  pl.BlockSpec((1, d), const),
                  pl.BlockSpec((d, LANES), const),
                  pl.BlockSpec((1, LANES), const)],
        out_specs=[pl.BlockSpec((tm, d), row),
                   pl.BlockSpec((tm * (d // 2 // LANES), LANES), row),
                   pl.BlockSpec((tm, LANES), row),
                   pl.BlockSpec((tm, LANES), row),
                   pl.BlockSpec((tm, LANES), row),
                   pl.BlockSpec((8, LANES), const)],
        out_shape=[jax.ShapeDtypeStruct((n, d), F32),
                   jax.ShapeDtypeStruct((n * (d // 2 // LANES), LANES), jnp.uint32),
                   jax.ShapeDtypeStruct((n, LANES), I32),
                   jax.ShapeDtypeStruct((n, LANES), F32),
                   jax.ShapeDtypeStruct((n, LANES), I32),
                   jax.ShapeDtypeStruct((8, LANES), F32)],
        scratch_shapes=[pltpu.VMEM((8, LANES), F32)],
        compiler_params=_params(1, 56),
        name="outproj_router",
    )(att, hm, x2d, wa, wm, g, rw, rb)


def _split_glu_body(w_ref, g_ref, l_ref):
    grp = 2 * LANES
    r = lax.broadcasted_iota(I32, (grp, grp), 0)
    c = lax.broadcasted_iota(I32, (grp, grp), 1)
    src = jnp.where(c < LANES, 2 * c, 2 * (c - LANES) + 1)
    perm = jnp.where(r == src, 1.0, 0.0).astype(BF16)
    for j in range(w_ref.shape[2] // grp):
        blk = w_ref[0, :, j * grp:(j + 1) * grp].astype(BF16)
        out = jnp.dot(blk, perm, preferred_element_type=F32).astype(BF16)
        g_ref[0, :, j * LANES:(j + 1) * LANES] = out[:, :LANES]
        l_ref[0, :, j * LANES:(j + 1) * LANES] = out[:, LANES:]


def _split_glu(w_up, tr, tc):
    e, d, f2 = w_up.shape
    out = jax.ShapeDtypeStruct((e, d, f2 // 2), BF16)
    return pl.pallas_call(
        _split_glu_body,
        grid=(e, d // tr, f2 // tc),
        in_specs=[pl.BlockSpec((1, tr, tc), lambda a, i, j: (a, i, j))],
        out_specs=[pl.BlockSpec((1, tr, tc // 2), lambda a, i, j: (a, i, j)),
                   pl.BlockSpec((1, tr, tc // 2), lambda a, i, j: (a, i, j))],
        out_shape=[out, out],
        compiler_params=_params(3, 48),
        name="split_glu_weights",
    )(w_up)


MOE_SUB = 512


def _moe_body(te_ref, tv_ref, to_ref, tn_ref, rowtok_hbm, h2p_hbm, wg_ref, wl_ref, bg_ref, bl_ref, wd_ref, bd_ref,
              ys_ref, idx_smem, xp_buf, xb_buf, sem_idx, sem_rows):
    i = pl.program_id(0)
    f = pl.program_id(1)
    nt = pl.num_programs(0)
    tm, half = xb_buf.shape[0], xb_buf.shape[1] // 2
    spr = half // LANES
    slot = lax.rem(i, 2)
    nslot = 1 - slot
    valid = tv_ref[i] > 0
    next_valid = jnp.logical_and(i + 1 < nt, tv_ref[jnp.minimum(i + 1, nt - 1)] > 0)

    def idx_copy(tile, s):
        return pltpu.make_async_copy(rowtok_hbm.at[tile], idx_smem.at[s], sem_idx.at[s])

    def issue_rows(s):
        def issue(r, carry):
            tok = idx_smem[s, r]
            pltpu.make_async_copy(h2p_hbm.at[pl.ds(pl.multiple_of(tok * spr, spr), spr), :],
                                  xp_buf.at[s, pl.ds(pl.multiple_of(r * spr, spr), spr), :],
                                  sem_rows.at[s]).start()
            return carry
        lax.fori_loop(0, tm, issue, 0, unroll=8)

    @pl.when(jnp.logical_and(jnp.logical_and(i == 0, f == 0), valid))
    def _():
        cp = idx_copy(0, 0)
        cp.start()
        cp.wait()
        issue_rows(0)

    @pl.when(jnp.logical_and(valid, f == 0))
    def _():
        pltpu.make_async_copy(h2p_hbm.at[pl.ds(0, tm * spr), :], xp_buf.at[slot], sem_rows.at[slot]).wait()
        for c in range(spr):
            u = xp_buf.at[slot][pl.ds(c, tm, stride=spr), :]
            xb_buf[:, c * LANES:(c + 1) * LANES] = lax.bitcast_convert_type(
                lax.shift_left(u, jnp.uint32(16)), F32).astype(BF16)
            xb_buf[:, half + c * LANES:half + (c + 1) * LANES] = lax.bitcast_convert_type(
                u & jnp.uint32(0xFFFF0000), F32).astype(BF16)

        @pl.when(next_valid)
        def _():
            idx_copy(i + 1, nslot).start()

    @pl.when(jnp.logical_and(jnp.logical_and(valid, f == 1), next_valid))
    def _():
        idx_copy(i + 1, nslot).wait()
        issue_rows(nslot)

    @pl.when(valid)
    def _():
        nrows = tn_ref[i]
        for sb in range(tm // MOE_SUB):
            rows = slice(sb * MOE_SUB, (sb + 1) * MOE_SUB)

            @pl.when(sb * MOE_SUB < nrows)
            def _(rows=rows):
                x = xb_buf[rows, :]
                hg = jnp.dot(x, wg_ref[0], preferred_element_type=F32) + bg_ref[0]
                hl = jnp.dot(x, wl_ref[0], preferred_element_type=F32) + bl_ref[0]
                glu = jnp.minimum(hg, SWIGLU_LIMIT)
                lin = jnp.clip(hl, -SWIGLU_LIMIT, SWIGLU_LIMIT)
                act = glu * jax.nn.sigmoid(SWIGLU_ALPHA * glu) * (lin + 1.0)
                y = jnp.dot(act.astype(BF16), wd_ref[0], preferred_element_type=F32)

                @pl.when(f == 0)
                def _():
                    ys_ref[rows, :] = y + bd_ref[0]

                @pl.when(f > 0)
                def _():
                    ys_ref[rows, :] += y


def _moe(tile_e, tile_valid, tile_out, tile_rows, row_tok, h2p, wg, wl, bg, bl, wd, bd, tm, fc):
    nt = row_tok.shape[0]
    d = wg.shape[1]
    half = d // 2
    assert h2p.shape[1] == LANES and half % LANES == 0
    ff = wg.shape[2]
    nf = ff // fc
    assert nf >= 2 and tm % MOE_SUB == 0

    def fidx(i, f, tv):
        return jnp.where(tv[i] > 0, f, nf - 1)

    grid_spec = pltpu.PrefetchScalarGridSpec(
        num_scalar_prefetch=4,
        grid=(nt, nf),
        in_specs=[pl.BlockSpec(memory_space=pl.ANY),
                  pl.BlockSpec(memory_space=pl.ANY),
                  pl.BlockSpec((1, d, fc), lambda i, f, te, tv, to, tn: (te[i], 0, fidx(i, f, tv))),
                  pl.BlockSpec((1, d, fc), lambda i, f, te, tv, to, tn: (te[i], 0, fidx(i, f, tv))),
                  pl.BlockSpec((1, 1, fc), lambda i, f, te, tv, to, tn: (te[i], 0, fidx(i, f, tv))),
                  pl.BlockSpec((1, 1, fc), lambda i, f, te, tv, to, tn: (te[i], 0, fidx(i, f, tv))),
                  pl.BlockSpec((1, fc, d), lambda i, f, te, tv, to, tn: (te[i], fidx(i, f, tv), 0)),
                  pl.BlockSpec((1, 1, d), lambda i, f, te, tv, to, tn: (te[i], 0, 0))],
        out_specs=pl.BlockSpec((tm, d), lambda i, f, te, tv, to, tn: (to[i], 0)),
        scratch_shapes=[pltpu.SMEM((2, tm), I32),
                        pltpu.VMEM((2, tm * (half // LANES), LANES), jnp.uint32),
                        pltpu.VMEM((tm, d), BF16),
                        pltpu.SemaphoreType.DMA((2,)),
                        pltpu.SemaphoreType.DMA((2,))],
    )
    return pl.pallas_call(
        _moe_body,
        grid_spec=grid_spec,
        out_shape=jax.ShapeDtypeStruct((nt * tm, d), F32),
        compiler_params=_params(2, 56),
        name="moe_ffn",
    )(tile_e, tile_valid, tile_out, tile_rows, row_tok, h2p, wg, wl, bg, bl, wd, bd)


def _combine_body(dest_hbm, ys_hbm, x1_ref, gate_ref, o_ref, idx_smem, buf, sem_idx, sem_rows):
    i = pl.program_id(0)
    tm = x1_ref.shape[0]
    cp = pltpu.make_async_copy(dest_hbm.at[i], idx_smem, sem_idx)
    cp.start()
    cp.wait()
    for k in range(TOP_K):
        def issue(r, carry, k=k):
            row = idx_smem[k * tm + r]
            pltpu.make_async_copy(ys_hbm.at[pl.ds(row, 1), :], buf.at[k, pl.ds(r, 1), :], sem_rows).start()
            return carry
        lax.fori_loop(0, tm, issue, 0, unroll=8)
    for k in range(TOP_K):
        pltpu.make_async_copy(ys_hbm.at[pl.ds(0, tm), :], buf.at[k], sem_rows).wait()
    out = x1_ref[...]
    g = gate_ref[...]
    for k in range(TOP_K):
        out = out + g[:, k:k + 1] * buf[k]
    o_ref[...] = out


def _combine(dest_tiles, ys, x1, gates, tm):
    n, d = x1.shape
    return pl.pallas_call(
        _combine_body,
        grid=(n // tm,),
        in_specs=[pl.BlockSpec(memory_space=pl.ANY),
                  pl.BlockSpec(memory_space=pl.ANY),
                  pl.BlockSpec((tm, d), lambda i: (i, 0)),
                  pl.BlockSpec((tm, LANES), lambda i: (i, 0))],
        out_specs=pl.BlockSpec((tm, d), lambda i: (i, 0)),
        out_shape=jax.ShapeDtypeStruct((n, d), F32),
        scratch_shapes=[pltpu.SMEM((TOP_K * tm,), I32),
                        pltpu.VMEM((TOP_K, tm, d), F32),
                        pltpu.SemaphoreType.DMA,
                        pltpu.SemaphoreType.DMA],
        compiler_params=_params(1, 40),
        name="moe_combine",
    )(dest_tiles, ys, x1, gates)


def _pad_lanes(a, width=LANES):
    return jnp.pad(a, [(0, 0)] * (a.ndim - 1) + [(0, width - a.shape[-1])])


def kernel(x, attn_norm_g, w_in, q_norm_g, k_norm_g, ml_conv_w, ml_conv_b, ml_i_b, ml_f_b, ml_out_norm_g,
           w_out, ffn_norm_g, router_w, router_b, w_up, b_up, w_down, b_down):
    B, S, D = x.shape
    N = B * S
    assert S % ATT_TILE == 0 and attn_norm_g.shape[0] == 1
    att_w = ATT_HEADS * HEAD_DIM
    main_w = 3 * att_w + 2 * ML_HEADS * ML_QK + 2 * ML_HEADS * ML_V
    x2d = x.reshape(N, D)

    w_in0 = w_in[0]
    w_main = w_in0[:, :main_w].astype(BF16)
    w_gate = _pad_lanes(w_in0[:, main_w:]).astype(BF16)
    proj, gates = _inproj(x2d, attn_norm_g[0][None, :], w_main, w_gate, tm=1024, tn=1024)

    att = _attention(proj, q_norm_g[0][None, :], k_norm_g[0][None, :], _rope_table(S), B, S)

    gate_bias = _pad_lanes(jnp.concatenate([ml_i_b[0], ml_f_b[0]])[None, :]).astype(F32)
    hm = _mlstm(proj, gates, gate_bias, ml_conv_w[0], ml_conv_b[0][None, :], ml_out_norm_g[0][None, :], B, S)

    w_out0 = w_out[0].astype(BF16)
    x1, h2, top_idx, top_gate, top_rank, counts = _outproj(
        att, hm, x2d, w_out0[:att_w], w_out0[att_w:], ffn_norm_g[0][None, :],
        _pad_lanes(router_w[0]).astype(F32), _pad_lanes(router_b[0][None, :]).astype(F32), tm=512)

    tm_moe = 1024
    n_tiles = N * TOP_K // tm_moe + N_EXPERTS
    cnt = counts[0, :N_EXPERTS].astype(I32)
    ptiles = (cnt + tm_moe - 1) // tm_moe
    tend = jnp.cumsum(ptiles)
    tstart = tend - ptiles
    total = tend[-1]
    pstart = tstart * tm_moe
    e_idx = top_idx[:, :TOP_K]
    dest = pstart[e_idx] + top_rank[:, :TOP_K]
    tok = jnp.broadcast_to(jnp.arange(N, dtype=I32)[:, None], (N, TOP_K))
    row_tok = jnp.zeros((n_tiles * tm_moe,), I32).at[dest.reshape(-1)].set(tok.reshape(-1))
    tile_ids = jnp.arange(n_tiles, dtype=I32)
    tile_e = jnp.minimum(jnp.sum((tile_ids[:, None] >= tend[None, :]).astype(I32), axis=1), N_EXPERTS - 1)
    tile_valid = (tile_ids < total).astype(I32)
    tile_out = jnp.clip(tile_ids, 0, jnp.maximum(total - 1, 0)).astype(I32)
    tile_rows = jnp.clip(cnt[tile_e] - (tile_ids - tstart[tile_e]) * tm_moe, 0, tm_moe).astype(I32)

    wg, wl = _split_glu(w_up[0], tr=1024, tc=2048)
    bg = b_up[0][:, None, 0::2].astype(F32)
    bl = b_up[0][:, None, 1::2].astype(F32)
    wd = w_down[0].astype(BF16)
    bd = b_down[0][:, None, :].astype(F32)
    ys = _moe(tile_e, tile_valid, tile_out, tile_rows, row_tok.reshape(n_tiles, tm_moe), h2, wg, wl, bg, bl, wd, bd,
              tm=tm_moe, fc=512)

    tm_c = 256
    dest_tiles = dest.reshape(N // tm_c, tm_c, TOP_K).transpose(0, 2, 1).reshape(N // tm_c, TOP_K * tm_c)
    out = _combine(dest_tiles, ys, x1, top_gate, tm=tm_c)
    return out.reshape(B, S, D)
```

```python
import functools
import math

import jax
import jax.numpy as jnp
from jax import lax
from jax.experimental import pallas as pl
from jax.experimental.pallas import tpu as pltpu

F32 = jnp.float32
BF16 = jnp.bfloat16
I32 = jnp.int32

NORM_EPS = 1e-6
ATT_HEADS = 8
HEAD_DIM = 128
ROPE_DIM = 32
ROPE_THETA = 500000.0
DILATIONS = (1, 4, 16)
BAND = 128
ATT_TILE = 2048
ML_HEADS = 4
ML_QK = 128
ML_V = 256
ML_CHUNK = 128
ML_CONV = 4
N_EXPERTS = 32
TOP_K = 4
SWIGLU_ALPHA = 1.702
SWIGLU_LIMIT = 7.0
NEG = -1e30
LANES = 128
MIB = 2 ** 20


def _params(n_axes, vmem_mib):
    return pltpu.CompilerParams(dimension_semantics=("arbitrary",) * n_axes,
                                vmem_limit_bytes=vmem_mib * MIB)


def _inproj_body(x_ref, g_ref, w_ref, wg_ref, o_ref, gate_ref, h_ref):
    @pl.when(pl.program_id(1) == 0)
    def _():
        xf = x_ref[...]
        ms = jnp.mean(xf * xf, axis=-1, keepdims=True)
        h = (xf * lax.rsqrt(ms + NORM_EPS) * g_ref[...]).astype(BF16)
        h_ref[...] = h
        gate_ref[...] = jnp.dot(h, wg_ref[...], preferred_element_type=F32)

    o_ref[...] = jnp.dot(h_ref[...], w_ref[...], preferred_element_type=F32)


def _inproj(x2d, g, w_main, w_gate, tm, tn):
    n, d = x2d.shape
    wcols = w_main.shape[1]
    return pl.pallas_call(
        _inproj_body,
        grid=(n // tm, wcols // tn),
        in_specs=[pl.BlockSpec((tm, d), lambda i, j: (i, 0)),
                  pl.BlockSpec((1, d), lambda i, j: (0, 0)),
                  pl.BlockSpec((d, tn), lambda i, j: (0, j)),
                  pl.BlockSpec((d, LANES), lambda i, j: (0, 0))],
        out_specs=[pl.BlockSpec((tm, tn), lambda i, j: (i, j)),
                   pl.BlockSpec((tm, LANES), lambda i, j: (i, 0))],
        out_shape=[jax.ShapeDtypeStruct((n, wcols), F32),
                   jax.ShapeDtypeStruct((n, LANES), F32)],
        scratch_shapes=[pltpu.VMEM((tm, d), BF16)],
        compiler_params=_params(2, 48),
        name="inproj",
    )(x2d, g, w_main, w_gate)


def _attn_body(q_ref, kc_ref, kp_ref, vc_ref, vp_ref, gq_ref, gk_ref, tc_ref, tp_ref, o_ref,
               qb, kb, vb, acc_s, m_s, l_s):
    t = pl.program_id(1)
    T = ATT_TILE
    CH = 256
    scale = 1.0 / math.sqrt(HEAD_DIM)

    ones = jnp.ones((HEAD_DIM, LANES), BF16)

    def norm_rope(x, g, tab):
        sq = x * x
        hi = sq.astype(BF16)
        lo = (sq - hi.astype(F32)).astype(BF16)
        ssq = jnp.dot(hi, ones, preferred_element_type=F32) + jnp.dot(lo, ones, preferred_element_type=F32)
        y = x * lax.rsqrt(ssq * (1.0 / HEAD_DIM) + NORM_EPS) * g
        c = tab[:, 0:LANES]
        s1 = tab[:, LANES:2 * LANES]
        s2 = tab[:, 2 * LANES:3 * LANES]
        return y * c + pltpu.roll(y, LANES - ROPE_DIM // 2, 1) * s1 + pltpu.roll(y, ROPE_DIM // 2, 1) * s2

    def prep(c, carry):
        rows = pl.ds(pl.multiple_of(c * CH, CH), CH)
        prow = pl.ds(pl.multiple_of(c * CH + T, CH), CH)
        tabc = tc_ref[rows, :]
        qb[rows, :] = norm_rope(q_ref[rows, :], gq_ref[...], tabc) * scale
        kb[prow, :] = norm_rope(kc_ref[rows, :], gk_ref[...], tabc)
        kb[rows, :] = norm_rope(kp_ref[rows, :], gk_ref[...], tp_ref[rows, :])
        vb[rows, :] = vp_ref[rows, :]
        vb[prow, :] = vc_ref[rows, :]
        return carry

    lax.fori_loop(0, T // CH, prep, 0)

    def unit(bidx, d, u):
        seg = BAND * d
        if d == 1:
            sg, qoff = u, pl.multiple_of(u * BAND, BAND)
        elif d == DILATIONS[-1]:
            sg, qoff = 0, u
        else:
            sg = lax.shift_right_logical(u, 2)
            qoff = sg * seg + lax.bitwise_and(u, d - 1)

        def rows(off):
            return pl.ds(off, BAND) if d == 1 else pl.ds(off, BAND, stride=d)

        qs = qb[rows(qoff), :].astype(BF16)
        kk = jnp.concatenate([kb[rows(T + qoff - seg), :], kb[rows(T + qoff), :]], axis=0).astype(BF16)
        vv = jnp.concatenate([vb[rows(T + qoff - seg), :], vb[rows(T + qoff), :]], axis=0).astype(BF16)
        s = lax.dot_general(qs, kk, (((1,), (1,)), ((), ())), preferred_element_type=F32)
        ii = lax.broadcasted_iota(I32, (BAND, 2 * BAND), 0)
        jj = lax.broadcasted_iota(I32, (BAND, 2 * BAND), 1)
        dist = BAND + ii - jj
        first = jnp.logical_and(t == 0, sg == 0)
        jmin = jnp.where(first, BAND, 0)
        valid = (dist >= 0) & (dist <= BAND) & (jj >= jmin)
        s = jnp.where(valid, s, NEG)
        m = jnp.max(s, axis=-1, keepdims=True)
        p = jnp.exp(s - m)
        l = jnp.sum(p, axis=-1, keepdims=True)
        acc = jnp.dot(p.astype(BF16), vv, preferred_element_type=F32)
        acc_s.at[bidx][rows(qoff), :] = acc
        m_s.at[bidx][rows(qoff), :] = jnp.broadcast_to(m, (BAND, LANES))
        l_s.at[bidx][rows(qoff), :] = jnp.broadcast_to(l, (BAND, LANES))

    for bidx, d in enumerate(DILATIONS):
        def body(u, carry, bidx=bidx, d=d):
            unit(bidx, d, u)
            return carry
        lax.fori_loop(0, T // BAND, body, 0, unroll=16)

    def mix(c, carry):
        rows = pl.ds(pl.multiple_of(c * CH, CH), CH)
        ms = [m_s[b, rows, :] for b in range(3)]
        mx = jnp.maximum(jnp.maximum(ms[0], ms[1]), ms[2])
        num = jnp.zeros((CH, LANES), F32)
        den = jnp.zeros((CH, LANES), F32)
        for b in range(3):
            w = jnp.exp(ms[b] - mx)
            num = num + w * acc_s[b, rows, :]
            den = den + w * l_s[b, rows, :]
        o_ref[rows, :] = (num / den).astype(o_ref.dtype)
        return carry

    lax.fori_loop(0, T // CH, mix, 0)


def _attention(proj, gq, gk, tab, batch, seq):
    n = proj.shape[0]
    T = ATT_TILE
    nt = seq // T
    H = ATT_HEADS

    def cur(col0):
        return pl.BlockSpec((T, HEAD_DIM), lambda b, t, h: (b * nt + t, col0 + h))

    def prev(col0):
        return pl.BlockSpec((T, HEAD_DIM), lambda b, t, h: (b * nt + jnp.maximum(t - 1, 0), col0 + h))

    return pl.pallas_call(
        _attn_body,
        grid=(batch, nt, H),
        in_specs=[cur(0), cur(H), prev(H), cur(2 * H), prev(2 * H),
                  pl.BlockSpec((1, HEAD_DIM), lambda b, t, h: (0, 0)),
                  pl.BlockSpec((1, HEAD_DIM), lambda b, t, h: (0, 0)),
                  pl.BlockSpec((T, 3 * LANES), lambda b, t, h: (t, 0)),
                  pl.BlockSpec((T, 3 * LANES), lambda b, t, h: (jnp.maximum(t - 1, 0), 0))],
        out_specs=pl.BlockSpec((T, HEAD_DIM), lambda b, t, h: (b * nt + t, h)),
        out_shape=jax.ShapeDtypeStruct((n, H * HEAD_DIM), BF16),
        scratch_shapes=[pltpu.VMEM((T, HEAD_DIM), F32),
                        pltpu.VMEM((2 * T, HEAD_DIM), F32),
                        pltpu.VMEM((2 * T, HEAD_DIM), F32),
                        pltpu.VMEM((3, T, HEAD_DIM), F32),
                        pltpu.VMEM((3, T, LANES), F32),
                        pltpu.VMEM((3, T, LANES), F32)],
        compiler_params=_params(3, 56),
        name="dilated_attn",
    )(proj, proj, proj, proj, proj, gq, gk, tab, tab)


def _rope_table(seq):
    half = ROPE_DIM // 2
    inv = ROPE_THETA ** (-jnp.arange(half, dtype=F32) * 2.0 / ROPE_DIM)
    ang = jnp.arange(seq, dtype=F32)[:, None] * inv[None, :]
    cos, sin = jnp.cos(ang), jnp.sin(ang)
    z = lambda w: jnp.zeros((seq, w), F32)
    c = jnp.concatenate([cos, cos, jnp.ones((seq, LANES - ROPE_DIM), F32)], axis=1)
    s1 = jnp.concatenate([-sin, z(LANES - half)], axis=1)
    s2 = jnp.concatenate([z(half), sin, z(LANES - ROPE_DIM)], axis=1)
    return jnp.concatenate([c, s1, s2], axis=1)


def _mlstm_body(mq_ref, mk_ref, mv_ref, mo_ref, gt_ref, gb_ref, cw_ref, cb_ref, og_ref, o_ref,
                prev_s, c_s, n_s, m_s):
    L = ML_CHUNK
    QW = ML_HEADS * ML_QK

    @pl.when(pl.program_id(1) == 0)
    def _():
        prev_s[...] = jnp.zeros_like(prev_s)
        c_s[...] = jnp.zeros_like(c_s)
        n_s[...] = jnp.zeros_like(n_s)
        m_s[...] = jnp.zeros_like(m_s)

    cur = jnp.concatenate([mq_ref[...], mk_ref[...]], axis=1)
    prv = prev_s[...]
    rid = lax.broadcasted_iota(I32, (L, 2 * QW), 0)
    conv = cur * cw_ref[ML_CONV - 1:ML_CONV, :] + cb_ref[...]
    for s in range(1, ML_CONV):
        shifted = jnp.where(rid < s, pltpu.roll(prv, s, 0), pltpu.roll(cur, s, 0))
        conv = conv + shifted * cw_ref[ML_CONV - 1 - s:ML_CONV - s, :]
    prev_s[...] = cur
    qk = conv * jax.nn.sigmoid(conv)

    gates = gt_ref[...] + gb_ref[...]
    logf = jnp.minimum(gates, 0.0) - jnp.log(1.0 + jnp.exp(-jnp.abs(gates)))
    ti = lax.broadcasted_iota(I32, (L, L), 0)
    si = lax.broadcasted_iota(I32, (L, L), 1)
    causal = si <= ti
    tril = jnp.where(causal, 1.0, 0.0).astype(F32)
    bmat = jnp.dot(tril, logf, precision=lax.Precision.HIGHEST, preferred_element_type=F32)
    rmat = gates - pltpu.roll(bmat, LANES - ML_HEADS, 1)
    rmat_t = rmat.T

    for h in range(ML_HEADS):
        q = qk[:, h * ML_QK:(h + 1) * ML_QK]
        k = qk[:, QW + h * ML_QK:QW + (h + 1) * ML_QK] * (1.0 / math.sqrt(ML_QK))
        v = mv_ref[:, h * ML_V:(h + 1) * ML_V]
        qb, kb, vb = q.astype(BF16), k.astype(BF16), v.astype(BF16)
        b_col = bmat[:, ML_HEADS + h:ML_HEADS + h + 1]
        r_col = rmat[:, h:h + 1]
        r_row = rmat_t[h:h + 1, :]
        g_tot = b_col[L - 1:L, :]
        m_prev = m_s[h]
        c_prev = c_s[h]
        n_prev = n_s[h]

        dmat = jnp.where(causal, b_col + r_row, NEG)
        inter = b_col + m_prev
        m_t = jnp.maximum(inter, jnp.max(dmat, axis=-1, keepdims=True))
        qkt = lax.dot_general(qb, kb, (((1,), (1,)), ((), ())), preferred_element_type=F32)
        pw = jnp.exp(dmat - m_t) * qkt
        wi = jnp.exp(inter - m_t)
        num = wi * jnp.dot(qb, c_prev.astype(BF16), preferred_element_type=F32) \
            + jnp.dot(pw.astype(BF16), vb, preferred_element_type=F32)
        den = wi * jnp.sum(q * n_prev, axis=-1, keepdims=True) + jnp.sum(pw, axis=-1, keepdims=True)
        hh = num / jnp.maximum(jnp.abs(den), jnp.exp(-m_t))
        ms = jnp.mean(hh * hh, axis=-1, keepdims=True)
        hn = hh * lax.rsqrt(ms + NORM_EPS) * og_ref[:, h * ML_V:(h + 1) * ML_V]
        hn = hn * jax.nn.sigmoid(mo_ref[:, h * ML_V:(h + 1) * ML_V])
        o_ref[:, h * ML_V:(h + 1) * ML_V] = hn.astype(o_ref.dtype)

        a_col = g_tot + r_col
        m_loc = jnp.max(a_col, axis=0, keepdims=True)
        wk = jnp.exp(a_col - m_loc)
        wkk = wk * k
        c_loc = lax.dot_general(wkk.astype(BF16), vb, (((0,), (0,)), ((), ())), preferred_element_type=F32)
        n_loc = jnp.sum(wkk, axis=0, keepdims=True)
        m_new = jnp.maximum(g_tot + m_prev, m_loc)
        d_old = jnp.exp(g_tot + m_prev - m_new)
        d_new = jnp.exp(m_loc - m_new)
        c_s[h] = d_old * c_prev + d_new * c_loc
        n_s[h] = d_old * n_prev + d_new * n_loc
        m_s[h] = m_new


def _mlstm(proj, gates, gate_bias, conv_w, conv_b, out_g, batch, seq):
    n = proj.shape[0]
    L = ML_CHUNK
    nc = seq // L
    QW = ML_HEADS * ML_QK
    VW = ML_HEADS * ML_V
    q_col = 3 * ATT_HEADS * HEAD_DIM
    assert q_col % QW == 0 and (q_col + 2 * QW) % VW == 0
    qi = q_col // QW
    vi = (q_col + 2 * QW) // VW
    row = lambda b, c: b * nc + c
    return pl.pallas_call(
        _mlstm_body,
        grid=(batch, nc),
        in_specs=[pl.BlockSpec((L, QW), lambda b, c: (row(b, c), qi)),
                  pl.BlockSpec((L, QW), lambda b, c: (row(b, c), qi + 1)),
                  pl.BlockSpec((L, VW), lambda b, c: (row(b, c), vi)),
                  pl.BlockSpec((L, VW), lambda b, c: (row(b, c), vi + 1)),
                  pl.BlockSpec((L, LANES), lambda b, c: (row(b, c), 0)),
                  pl.BlockSpec((1, LANES), lambda b, c: (0, 0)),
                  pl.BlockSpec((ML_CONV, 2 * QW), lambda b, c: (0, 0)),
                  pl.BlockSpec((1, 2 * QW), lambda b, c: (0, 0)),
                  pl.BlockSpec((1, VW), lambda b, c: (0, 0))],
        out_specs=pl.BlockSpec((L, VW), lambda b, c: (row(b, c), 0)),
        out_shape=jax.ShapeDtypeStruct((n, VW), BF16),
        scratch_shapes=[pltpu.VMEM((L, 2 * QW), F32),
                        pltpu.VMEM((ML_HEADS, ML_QK, ML_V), F32),
                        pltpu.VMEM((ML_HEADS, 1, ML_QK), F32),
                        pltpu.VMEM((ML_HEADS, 1, 1), F32)],
        compiler_params=_params(2, 32),
        name="mlstm",
    )(proj, proj, proj, proj, gates, gate_bias, conv_w, conv_b, out_g)


def _outproj_body(att_ref, hm_ref, x_ref, wa_ref, wm_ref, g_ref, rw_ref, rb_ref,
                  x1_ref, h2_ref, idx_ref, gate_ref, rank_ref, cnt_ref, carry):
    tm = x_ref.shape[0]

    @pl.when(pl.program_id(0) == 0)
    def _():
        carry[...] = jnp.zeros_like(carry)

    y = jnp.dot(att_ref[...], wa_ref[...], preferred_element_type=F32)
    y = y + jnp.dot(hm_ref[...], wm_ref[...], preferred_element_type=F32)
    x1 = x_ref[...] + y
    x1_ref[...] = x1
    ms = jnp.mean(x1 * x1, axis=-1, keepdims=True)
    h2 = x1 * lax.rsqrt(ms + NORM_EPS) * g_ref[...]
    half = h2.shape[1] // 2
    hb = h2.astype(BF16).astype(F32)
    lo = lax.shift_right_logical(lax.bitcast_convert_type(hb[:, :half], jnp.uint32), jnp.uint32(16))
    hi = lax.bitcast_convert_type(hb[:, half:], jnp.uint32) & jnp.uint32(0xFFFF0000)
    packed = lo | hi
    for s in range(half // LANES):
        h2_ref[pl.ds(s, tm, stride=half // LANES), :] = packed[:, s * LANES:(s + 1) * LANES]
    logits = jnp.dot(h2, rw_ref[...], precision=lax.Precision.HIGHEST,
                     preferred_element_type=F32) + rb_ref[...]
    lane = lax.broadcasted_iota(I32, (tm, LANES), 1)
    work = jnp.where(lane < N_EXPERTS, logits, NEG)
    vals, idxs = [], []
    onehot = jnp.zeros((tm, LANES), F32)
    for _ in range(TOP_K):
        mx = jnp.max(work, axis=-1, keepdims=True)
        ix = jnp.min(jnp.where(work == mx, lane, LANES), axis=-1, keepdims=True)
        sel = lane == ix
        onehot = jnp.where(sel, 1.0, onehot)
        work = jnp.where(sel, 2.0 * NEG, work)
        vals.append(mx)
        idxs.append(ix)
    es = [jnp.exp(v - vals[0]) for v in vals]
    tot = es[0] + es[1] + es[2] + es[3]

    ri = lax.broadcasted_iota(I32, (tm, tm), 0)
    ci = lax.broadcasted_iota(I32, (tm, tm), 1)
    strict = jnp.where(ci < ri, 1.0, 0.0).astype(BF16)
    cum = jnp.dot(strict, onehot.astype(BF16), preferred_element_type=F32) + carry[0:1, :]
    idx_out = jnp.zeros((tm, LANES), I32)
    gate_out = jnp.zeros((tm, LANES), F32)
    rank_out = jnp.zeros((tm, LANES), I32)
    for k in range(TOP_K):
        rk = jnp.sum(jnp.where(lane == idxs[k], cum, 0.0), axis=-1, keepdims=True)
        idx_out = jnp.where(lane == k, idxs[k], idx_out)
        gate_out = jnp.where(lane == k, es[k] / tot, gate_out)
        rank_out = jnp.where(lane == k, rk.astype(I32), rank_out)
    idx_ref[...] = idx_out
    gate_ref[...] = gate_out
    rank_ref[...] = rank_out
    new = carry[...] + jnp.sum(onehot, axis=0, keepdims=True)
    carry[...] = new
    cnt_ref[...] = new


def _outproj(att, hm, x2d, wa, wm, g, rw, rb, tm):
    n, d = x2d.shape
    row = lambda i: (i, 0)
    const = lambda i: (0, 0)
    return pl.pallas_call(
        _outproj_body,
        grid=(n // tm,),
        in_specs=[pl.BlockSpec((tm, att.shape[1]), row),
                  pl.BlockSpec((tm, hm.shape[1]), row),
                  pl.BlockSpec((tm, d), row),
                  pl.BlockSpec(wa.shape, const),
                  pl.BlockSpec(wm.shape, const),
                  pl.BlockSpec((1, d), const),
                  pl.BlockSpec((d, LANES), const),
                  pl.BlockSpec((1, LANES), const)],
        out_specs=[pl.BlockSpec((tm, d), row),
                   pl.BlockSpec((tm * (d // 2 // LANES), LANES), row),
                   pl.BlockSpec((tm, LANES), row),
                   pl.BlockSpec((tm, LANES), row),
                   pl.BlockSpec((tm, LANES), row),
                   pl.BlockSpec((8, LANES), const)],
        out_shape=[jax.ShapeDtypeStruct((n, d), F32),
                   jax.ShapeDtypeStruct((n * (d // 2 // LANES), LANES), jnp.uint32),
                   jax.ShapeDtypeStruct((n, LANES), I32),
                   jax.ShapeDtypeStruct((n, LANES), F32),
                   jax.ShapeDtypeStruct((n, LANES), I32),
                   jax.ShapeDtypeStruct((8, LANES), F32)],
        scratch_shapes=[pltpu.VMEM((8, LANES), F32)],
        compiler_params=_params(1, 56),
        name="outproj_router",
    )(att, hm, x2d, wa, wm, g, rw, rb)


def _split_glu_body(w_ref, g_ref, l_ref):
    grp = 2 * LANES
    r = lax.broadcasted_iota(I32, (grp, grp), 0)
    c = lax.broadcasted_iota(I32, (grp, grp), 1)
    src = jnp.where(c < LANES, 2 * c, 2 * (c - LANES) + 1)
    perm = jnp.where(r == src, 1.0, 0.0).astype(BF16)
    for j in range(w_ref.shape[2] // grp):
        blk = w_ref[0, :, j * grp:(j + 1) * grp].astype(BF16)
        out = jnp.dot(blk, perm, preferred_element_type=F32).astype(BF16)
        g_ref[0, :, j * LANES:(j + 1) * LANES] = out[:, :LANES]
        l_ref[0, :, j * LANES:(j + 1) * LANES] = out[:, LANES:]


def _split_glu(w_up, tr, tc):
    e, d, f2 = w_up.shape
    out = jax.ShapeDtypeStruct((e, d, f2 // 2), BF16)
    return pl.pallas_call(
        _split_glu_body,
        grid=(e, d // tr, f2 // tc),
        in_specs=[pl.BlockSpec((1, tr, tc), lambda a, i, j: (a, i, j))],
        out_specs=[pl.BlockSpec((1, tr, tc // 2), lambda a, i, j: (a, i, j)),
                   pl.BlockSpec((1, tr, tc // 2), lambda a, i, j: (a, i, j))],
        out_shape=[out, out],
        compiler_params=_params(3, 48),
        name="split_glu_weights",
    )(w_up)


MOE_SUB = 512


def _moe_body(te_ref, tv_ref, to_ref, tn_ref, rowtok_hbm, h2p_hbm, wg_ref, wl_ref, bg_ref, bl_ref, wd_ref, bd_ref,
              ys_ref, idx_smem, xp_buf, xb_buf, sem_idx, sem_rows):
    i = pl.program_id(0)
    f = pl.program_id(1)
    nt = pl.num_programs(0)
    tm, half = xb_buf.shape[0], xb_buf.shape[1] // 2
    spr = half // LANES
    slot = lax.rem(i, 2)
    nslot = 1 - slot
    valid = tv_ref[i] > 0
    next_valid = jnp.logical_and(i + 1 < nt, tv_ref[jnp.minimum(i + 1, nt - 1)] > 0)

    def idx_copy(tile, s):
        return pltpu.make_async_copy(rowtok_hbm.at[tile], idx_smem.at[s], sem_idx.at[s])

    def issue_rows(s):
        def issue(r, carry):
            tok = idx_smem[s, r]
            pltpu.make_async_copy(h2p_hbm.at[pl.ds(pl.multiple_of(tok * spr, spr), spr), :],
                                  xp_buf.at[s, pl.ds(pl.multiple_of(r * spr, spr), spr), :],
                                  sem_rows.at[s]).start()
            return carry
        lax.fori_loop(0, tm, issue, 0, unroll=8)

    @pl.when(jnp.logical_and(jnp.logical_and(i == 0, f == 0), valid))
    def _():
        cp = idx_copy(0, 0)
        cp.start()
        cp.wait()
        issue_rows(0)

    @pl.when(jnp.logical_and(valid, f == 0))
    def _():
        pltpu.make_async_copy(h2p_hbm.at[pl.ds(0, tm * spr), :], xp_buf.at[slot], sem_rows.at[slot]).wait()
        for c in range(spr):
            u = xp_buf.at[slot][pl.ds(c, tm, stride=spr), :]
            xb_buf[:, c * LANES:(c + 1) * LANES] = lax.bitcast_convert_type(
                lax.shift_left(u, jnp.uint32(16)), F32).astype(BF16)
            xb_buf[:, half + c * LANES:half + (c + 1) * LANES] = lax.bitcast_convert_type(
                u & jnp.uint32(0xFFFF0000), F32).astype(BF16)

        @pl.when(next_valid)
        def _():
            idx_copy(i + 1, nslot).start()

    @pl.when(jnp.logical_and(jnp.logical_and(valid, f == 1), next_valid))
    def _():
        idx_copy(i + 1, nslot).wait()
        issue_rows(nslot)

    @pl.when(valid)
    def _():
        nrows = tn_ref[i]
        for sb in range(tm // MOE_SUB):
            rows = slice(sb * MOE_SUB, (sb + 1) * MOE_SUB)

            @pl.when(sb * MOE_SUB < nrows)
            def _(rows=rows):
                x = xb_buf[rows, :]
                hg = jnp.dot(x, wg_ref[0], preferred_element_type=F32) + bg_ref[0]
                hl = jnp.dot(x, wl_ref[0], preferred_element_type=F32) + bl_ref[0]
                glu = jnp.minimum(hg, SWIGLU_LIMIT)
                lin = jnp.clip(hl, -SWIGLU_LIMIT, SWIGLU_LIMIT)
                act = glu * jax.nn.sigmoid(SWIGLU_ALPHA * glu) * (lin + 1.0)
                y = jnp.dot(act.astype(BF16), wd_ref[0], preferred_element_type=F32)

                @pl.when(f == 0)
                def _():
                    ys_ref[rows, :] = y + bd_ref[0]

                @pl.when(f > 0)
                def _():
                    ys_ref[rows, :] += y


def _moe(tile_e, tile_valid, tile_out, tile_rows, row_tok, h2p, wg, wl, bg, bl, wd, bd, tm, fc):
    nt = row_tok.shape[0]
    d = wg.shape[1]
    half = d // 2
    assert h2p.shape[1] == LANES and half % LANES == 0
    ff = wg.shape[2]
    nf = ff // fc
    assert nf >= 2 and tm % MOE_SUB == 0

    def fidx(i, f, tv):
        return jnp.where(tv[i] > 0, f, nf - 1)

    grid_spec = pltpu.PrefetchScalarGridSpec(
        num_scalar_prefetch=4,
        grid=(nt, nf),
        in_specs=[pl.BlockSpec(memory_space=pl.ANY),
                  pl.BlockSpec(memory_space=pl.ANY),
                  pl.BlockSpec((1, d, fc), lambda i, f, te, tv, to, tn: (te[i], 0, fidx(i, f, tv))),
                  pl.BlockSpec((1, d, fc), lambda i, f, te, tv, to, tn: (te[i], 0, fidx(i, f, tv))),
                  pl.BlockSpec((1, 1, fc), lambda i, f, te, tv, to, tn: (te[i], 0, fidx(i, f, tv))),
                  pl.BlockSpec((1, 1, fc), lambda i, f, te, tv, to, tn: (te[i], 0, fidx(i, f, tv))),
                  pl.BlockSpec((1, fc, d), lambda i, f, te, tv, to, tn: (te[i], fidx(i, f, tv), 0)),
                  pl.BlockSpec((1, 1, d), lambda i, f, te, tv, to, tn: (te[i], 0, 0))],
        out_specs=pl.BlockSpec((tm, d), lambda i, f, te, tv, to, tn: (to[i], 0)),
        scratch_shapes=[pltpu.SMEM((2, tm), I32),
                        pltpu.VMEM((2, tm * (half // LANES), LANES), jnp.uint32),
                        pltpu.VMEM((tm, d), BF16),
                        pltpu.SemaphoreType.DMA((2,)),
                        pltpu.SemaphoreType.DMA((2,))],
    )
    return pl.pallas_call(
        _moe_body,
        grid_spec=grid_spec,
        out_shape=jax.ShapeDtypeStruct((nt * tm, d), F32),
        compiler_params=_params(2, 56),
        name="moe_ffn",
    )(tile_e, tile_valid, tile_out, tile_rows, row_tok, h2p, wg, wl, bg, bl, wd, bd)


def _combine_body(dest_hbm, ys_hbm, x1_ref, gate_ref, o_ref, idx_smem, buf, sem_idx, sem_rows):
    i = pl.program_id(0)
    tm = x1_ref.shape[0]
    cp = pltpu.make_async_copy(dest_hbm.at[i], idx_smem, sem_idx)
    cp.start()
    cp.wait()
    for k in range(TOP_K):
        def issue(r, carry, k=k):
            row = idx_smem[k * tm + r]
            pltpu.make_async_copy(ys_hbm.at[pl.ds(row, 1), :], buf.at[k, pl.ds(r, 1), :], sem_rows).start()
            return carry
        lax.fori_loop(0, tm, issue, 0, unroll=8)
    for k in range(TOP_K):
        pltpu.make_async_copy(ys_hbm.at[pl.ds(0, tm), :], buf.at[k], sem_rows).wait()
    out = x1_ref[...]
    g = gate_ref[...]
    for k in range(TOP_K):
        out = out + g[:, k:k + 1] * buf[k]
    o_ref[...] = out


def _combine(dest_tiles, ys, x1, gates, tm):
    n, d = x1.shape
    return pl.pallas_call(
        _combine_body,
        grid=(n // tm,),
        in_specs=[pl.BlockSpec(memory_space=pl.ANY),
                  pl.BlockSpec(memory_space=pl.ANY),
                  pl.BlockSpec((tm, d), lambda i: (i, 0)),
                  pl.BlockSpec((tm, LANES), lambda i: (i, 0))],
        out_specs=pl.BlockSpec((tm, d), lambda i: (i, 0)),
        out_shape=jax.ShapeDtypeStruct((n, d), F32),
        scratch_shapes=[pltpu.SMEM((TOP_K * tm,), I32),
                        pltpu.VMEM((TOP_K, tm, d), F32),
                        pltpu.SemaphoreType.DMA,
                        pltpu.SemaphoreType.DMA],
        compiler_params=_params(1, 40),
        name="moe_combine",
    )(dest_tiles, ys, x1, gates)


def _pad_lanes(a, width=LANES):
    return jnp.pad(a, [(0, 0)] * (a.ndim - 1) + [(0, width - a.shape[-1])])


def kernel(x, attn_norm_g, w_in, q_norm_g, k_norm_g, ml_conv_w, ml_conv_b, ml_i_b, ml_f_b, ml_out_norm_g,
           w_out, ffn_norm_g, router_w, router_b, w_up, b_up, w_down, b_down):
    B, S, D = x.shape
    N = B * S
    assert S % ATT_TILE == 0 and attn_norm_g.shape[0] == 1
    att_w = ATT_HEADS * HEAD_DIM
    main_w = 3 * att_w + 2 * ML_HEADS * ML_QK + 2 * ML_HEADS * ML_V
    x2d = x.reshape(N, D)

    w_in0 = w_in[0]
    w_main = w_in0[:, :main_w].astype(BF16)
    w_gate = _pad_lanes(w_in0[:, main_w:]).astype(BF16)
    proj, gates = _inproj(x2d, attn_norm_g[0][None, :], w_main, w_gate, tm=1024, tn=1024)

    att = _attention(proj, q_norm_g[0][None, :], k_norm_g[0][None, :], _rope_table(S), B, S)

    gate_bias = _pad_lanes(jnp.concatenate([ml_i_b[0], ml_f_b[0]])[None, :]).astype(F32)
    hm = _mlstm(proj, gates, gate_bias, ml_conv_w[0], ml_conv_b[0][None, :], ml_out_norm_g[0][None, :], B, S)

    w_out0 = w_out[0].astype(BF16)
    x1, h2, top_idx, top_gate, top_rank, counts = _outproj(
        att, hm, x2d, w_out0[:att_w], w_out0[att_w:], ffn_norm_g[0][None, :],
        _pad_lanes(router_w[0]).astype(F32), _pad_lanes(router_b[0][None, :]).astype(F32), tm=512)

    tm_moe = 1024
    n_tiles = N * TOP_K // tm_moe + N_EXPERTS
    cnt = counts[0, :N_EXPERTS].astype(I32)
    ptiles = (cnt + tm_moe - 1) // tm_moe
    tend = jnp.cumsum(ptiles)
    tstart = tend - ptiles
    total = tend[-1]
    pstart = tstart * tm_moe
    e_idx = top_idx[:, :TOP_K]
    dest = pstart[e_idx] + top_rank[:, :TOP_K]
    tok = jnp.broadcast_to(jnp.arange(N, dtype=I32)[:, None], (N, TOP_K))
    row_tok = jnp.zeros((n_tiles * tm_moe,), I32).at[dest.reshape(-1)].set(tok.reshape(-1))
    tile_ids = jnp.arange(n_tiles, dtype=I32)
    tile_e = jnp.minimum(jnp.sum((tile_ids[:, None] >= tend[None, :]).astype(I32), axis=1), N_EXPERTS - 1)
    tile_valid = (tile_ids < total).astype(I32)
    tile_out = jnp.clip(tile_ids, 0, jnp.maximum(total - 1, 0)).astype(I32)
    tile_rows = jnp.clip(cnt[tile_e] - (tile_ids - tstart[tile_e]) * tm_moe, 0, tm_moe).astype(I32)

    wg, wl = _split_glu(w_up[0], tr=1024, tc=2048)
    bg = b_up[0][:, None, 0::2].astype(F32)
    bl = b_up[0][:, None, 1::2].astype(F32)
    wd = w_down[0].astype(BF16)
    bd = b_down[0][:, None, :].astype(F32)
    ys = _moe(tile_e, tile_valid, tile_out, tile_rows, row_tok.reshape(n_tiles, tm_moe), h2, wg, wl, bg, bl, wd, bd,
              tm=tm_moe, fc=512)

    tm_c = 256
    dest_tiles = dest.reshape(N // tm_c, tm_c, TOP_K).transpose(0, 2, 1).reshape(N // tm_c, TOP_K * tm_c)
    out = _combine(dest_tiles, ys, x1, top_gate, tm=tm_c)
    return out.reshape(B, S, D)
```
